```python
import math
import jax, jax.numpy as jnp
from jax import lax
import numpy as np

D_MODEL = 1024
BATCH = 8
SEQ = 8192
DEPTH = 4
DEC_BATCH = 1
DEC_SEQ = 16384
PAST_LEN = 128

BRANCH_WIDTH = D_MODEL // 2
N_BRANCHES = 3
FNET_GROUPS = 4
FNET_GROUP_DIM = BRANCH_WIDTH // FNET_GROUPS
SGU_CHUNK = 128
SGU_HEADS = 8
SGU_HEAD_DIM = BRANCH_WIDTH // SGU_HEADS
S5_GROUP_DIM = 16
S5_GROUPS = BRANCH_WIDTH // S5_GROUP_DIM
S5_STATE = 64
S5_DIRS = 2
GATE_WIDTH = N_BRANCHES * D_MODEL
SPLIT_A = BRANCH_WIDTH
SPLIT_B = SPLIT_A + 2 * BRANCH_WIDTH
SPLIT_C = SPLIT_B + BRANCH_WIDTH
IN_WIDTH = SPLIT_C + GATE_WIDTH
D_FF = 2816
CONV_WIDTH = 3
RMS_EPS = 1e-6
DT_MIN = 1e-3
DT_MAX = 1e-1

kernel_name = "hybrid_fnet_sgu_s5_encoder"


def rmsnorm(x, g):
    xf = x.astype(jnp.float32)
    xf = xf * lax.rsqrt(jnp.mean(xf * xf, axis=-1, keepdims=True) + RMS_EPS)
    return (xf * g.astype(jnp.float32)).astype(x.dtype)


def fourier_mix(a):
    bsz, s, _ = a.shape
    af = a.astype(jnp.float32).reshape(bsz, s, FNET_GROUPS, FNET_GROUP_DIM)
    f = jnp.fft.fft2(af, axes=(1, 3), norm="ortho")
    return jnp.real(f).reshape(bsz, s, BRANCH_WIDTH).astype(a.dtype)


def spatial_gating(uv, v_gain, w_s, b_s):
    bsz, s, _ = uv.shape
    uv = jax.nn.gelu(uv)
    u, v = jnp.split(uv, 2, axis=-1)
    v = rmsnorm(v, v_gain)
    v = v.reshape(bsz, s // SGU_CHUNK, SGU_CHUNK, SGU_HEADS, SGU_HEAD_DIM)
    mixed = jnp.einsum('hpq,bnqhc->bnphc', w_s.astype(v.dtype), v) + b_s.T[:, :, None].astype(v.dtype)
    return u * mixed.reshape(bsz, s, BRANCH_WIDTH)


def _ssm_combine(left, right):
    a_l, b_l = left
    a_r, b_r = right
    return a_r * a_l, a_r * b_l + b_r


def s5_mix(xc, lam_re, lam_im, log_dt, b_re, b_im, c_re, c_im, d_skip, w_glu):
    f32 = jnp.float32
    bsz, s, _ = xc.shape
    u = xc.astype(f32).reshape(bsz, s, S5_GROUPS, S5_GROUP_DIM)
    lam = lax.complex(lam_re.astype(f32), lam_im.astype(f32))
    dt = jnp.exp(log_dt.astype(f32))[..., None]
    lam_bar = jnp.exp(lam * dt)
    b_scale = (lam_bar - 1.0) / lam
    bmat = lax.complex(b_re.astype(f32), b_im.astype(f32))
    bu = jnp.einsum('bsgh,gph->bsgp', u, bmat)
    y = d_skip.astype(f32) * u
    for d in range(S5_DIRS):
        a = jnp.broadcast_to(lam_bar[d], bu.shape)
        _, states = lax.associative_scan(_ssm_combine, (a, bu * b_scale[d]), axis=1, reverse=(d == 1))
        cmat = lax.complex(c_re[d].astype(f32), c_im[d].astype(f32))
        y = y + jnp.real(jnp.einsum('bsgp,ghp->bsgh', states, cmat))
    y = jax.nn.gelu(y.reshape(bsz, s, BRANCH_WIDTH))
    z = y @ w_glu.astype(f32)
    val, gate = jnp.split(z, 2, axis=-1)
    return (val * jax.nn.sigmoid(gate)).astype(xc.dtype)


def conv_ffn(h, w_up, conv_w, conv_b, w_down):
    up = h @ w_up
    p = jnp.pad(up, ((0, 0), (1, 1), (0, 0)))
    up = p[:, :-2] * conv_w[0] + p[:, 1:-1] * conv_w[1] + p[:, 2:] * conv_w[2] + conv_b
    gate, val = jnp.split(up, 2, axis=-1)
    return (jax.nn.silu(gate) * val) @ w_down


def _layer(x, g_mix, w_in, w_fnet, sgu_v_gain, sgu_w, sgu_b, w_sgu, s5_lam_re, s5_lam_im, s5_log_dt,
           s5_b_re, s5_b_im, s5_c_re, s5_c_im, s5_d, w_glu, w_out, g_ffn, w_up, conv_w, conv_b, w_down):
    h = rmsnorm(x, g_mix)
    proj = h @ w_in
    a_in = proj[..., :SPLIT_A]
    uv_in = proj[..., SPLIT_A:SPLIT_B]
    c_in = proj[..., SPLIT_B:SPLIT_C]
    gates = jax.nn.sigmoid(proj[..., SPLIT_C:].astype(jnp.float32)).astype(x.dtype)
    ya = fourier_mix(a_in) @ w_fnet
    yb = spatial_gating(uv_in, sgu_v_gain, sgu_w, sgu_b) @ w_sgu
    yc = s5_mix(c_in, s5_lam_re, s5_lam_im, s5_log_dt, s5_b_re, s5_b_im, s5_c_re, s5_c_im, s5_d, w_glu)
    merged = (gates[..., :D_MODEL] * ya + gates[..., D_MODEL:2 * D_MODEL] * yb
              + gates[..., 2 * D_MODEL:] * yc)
    x = x + merged @ w_out
    x = x + conv_ffn(rmsnorm(x, g_ffn), w_up, conv_w, conv_b, w_down)
    return x


def _trunk(x, params, g_final):
    for i in range(DEPTH):
        x = _layer(x, *[p[i] for p in params])
    return rmsnorm(x, g_final)


def setup_inputs(seed: int = 0) -> dict:
    key = jax.random.key(seed)
    ks = jax.random.split(key, 25)
    f32 = jnp.float32

    def nrm(k, shape, scale):
        return jax.random.normal(k, shape, f32) * scale

    lam_shape = (DEPTH, S5_DIRS, S5_GROUPS, S5_STATE)
    n = jnp.arange(S5_STATE, dtype=f32)
    return {
        "x_prompt": nrm(ks[0], (BATCH, SEQ, D_MODEL), 1.0),
        "x_sample": nrm(ks[1], (DEC_BATCH, DEC_SEQ, D_MODEL), 1.0),
        "g_mix": 1.0 + nrm(ks[2], (DEPTH, D_MODEL), 0.02),
        "w_in": nrm(ks[3], (DEPTH, D_MODEL, IN_WIDTH), D_MODEL ** -0.5),
        "w_fnet": nrm(ks[4], (DEPTH, BRANCH_WIDTH, D_MODEL), BRANCH_WIDTH ** -0.5),
        "sgu_v_gain": 1.0 + nrm(ks[5], (DEPTH, BRANCH_WIDTH), 0.02),
        "sgu_w": nrm(ks[6], (DEPTH, SGU_HEADS, SGU_CHUNK, SGU_CHUNK), SGU_CHUNK ** -0.5),
        "sgu_b": 1.0 + nrm(ks[7], (DEPTH, SGU_HEADS, SGU_CHUNK), 0.02),
        "w_sgu": nrm(ks[8], (DEPTH, BRANCH_WIDTH, D_MODEL), BRANCH_WIDTH ** -0.5),
        "s5_lam_re": -0.5 + nrm(ks[9], lam_shape, 0.01),
        "s5_lam_im": jnp.pi * n + nrm(ks[10], lam_shape, 0.01),
        "s5_log_dt": jax.random.uniform(ks[11], (DEPTH, S5_DIRS, S5_GROUPS), f32,
                                        math.log(DT_MIN), math.log(DT_MAX)),
        "s5_b_re": nrm(ks[12], (DEPTH, S5_GROUPS, S5_STATE, S5_GROUP_DIM), (2 * S5_GROUP_DIM) ** -0.5),
        "s5_b_im": nrm(ks[13], (DEPTH, S5_GROUPS, S5_STATE, S5_GROUP_DIM), (2 * S5_GROUP_DIM) ** -0.5),
        "s5_c_re": nrm(ks[14], (DEPTH, S5_DIRS, S5_GROUPS, S5_GROUP_DIM, S5_STATE), S5_STATE ** -0.5),
        "s5_c_im": nrm(ks[15], (DEPTH, S5_DIRS, S5_GROUPS, S5_GROUP_DIM, S5_STATE), S5_STATE ** -0.5),
        "s5_d": nrm(ks[16], (DEPTH, S5_GROUPS, S5_GROUP_DIM), 0.5),
        "w_glu": nrm(ks[17], (DEPTH, BRANCH_WIDTH, 2 * D_MODEL), BRANCH_WIDTH ** -0.5),
        "w_out": nrm(ks[18], (DEPTH, D_MODEL, D_MODEL), D_MODEL ** -0.5),
        "g_ffn": 1.0 + nrm(ks[19], (DEPTH, D_MODEL), 0.02),
        "w_up": nrm(ks[20], (DEPTH, D_MODEL, 2 * D_FF), D_MODEL ** -0.5),
        "conv_w": nrm(ks[21], (DEPTH, CONV_WIDTH, 2 * D_FF), CONV_WIDTH ** -0.5),
        "conv_b": nrm(ks[22], (DEPTH, 2 * D_FF), 0.01),
        "w_down": nrm(ks[23], (DEPTH, D_FF, D_MODEL), D_FF ** -0.5),
        "g_final": 1.0 + nrm(ks[24], (D_MODEL,), 0.02),
    }


def reference(x_prompt, x_sample, g_mix, w_in, w_fnet, sgu_v_gain, sgu_w, sgu_b, w_sgu,
              s5_lam_re, s5_lam_im, s5_log_dt, s5_b_re, s5_b_im, s5_c_re, s5_c_im, s5_d,
              w_glu, w_out, g_ffn, w_up, conv_w, conv_b, w_down, g_final):
    params = (g_mix, w_in, w_fnet, sgu_v_gain, sgu_w, sgu_b, w_sgu, s5_lam_re, s5_lam_im, s5_log_dt,
              s5_b_re, s5_b_im, s5_c_re, s5_c_im, s5_d, w_glu, w_out, g_ffn, w_up, conv_w, conv_b, w_down)
    y_prompt = _trunk(x_prompt, params, g_final)
    y_sample = _trunk(x_sample, params, g_final)
    return (y_prompt, y_sample)
```

```python
import functools
import math

import numpy as np
import jax
import jax.numpy as jnp
from jax import lax
from jax.experimental import pallas as pl
from jax.experimental.pallas import tpu as pltpu

F32 = jnp.float32
BF16 = jnp.bfloat16

D_MODEL = 1024
BRANCH = 512
FNET_GROUP = 128
FNET_GROUPS = BRANCH // FNET_GROUP
SGU_CHUNK = 128
SGU_HEADS = 8
SGU_HEAD_DIM = BRANCH // SGU_HEADS
S5_H = 16
S5_G = BRANCH // S5_H
S5_P = 64
S5_L = 16
S5_Q = 4
S5_GQ = S5_G // S5_Q
S5_W = S5_L * FNET_GROUP
D_FF = 2816
FF_BLK = 256
FF_NBLK = D_FF // FF_BLK
RMS_EPS = 1e-6
FFT_N2 = 128
HALO = 16

TOKEN_TILE = 512
VMEM_LIMIT = 56 * 1024 * 1024


def _cparams(*sem):
    return pltpu.CompilerParams(dimension_semantics=sem, vmem_limit_bytes=VMEM_LIMIT)


def _const_spec(shape):
    nd = len(shape)
    return pl.BlockSpec(shape, lambda *_: (0,) * nd)


def _dot(a, b):
    return jnp.dot(a, b, preferred_element_type=F32)


def _rms(x, g):
    return x * lax.rsqrt(jnp.mean(x * x, axis=-1, keepdims=True) + RMS_EPS) * g


def _proj_kernel(x_ref, g_ref, w_ref, fc_ref, vg_ref, ws_ref, sb_ref,
                 xf_ref, sg_ref, c4_ref, *, tm):
    h = _rms(x_ref[...], g_ref[...]).astype(BF16)

    a = _dot(h, w_ref[:, 0:BRANCH]).astype(BF16)
    for g in range(FNET_GROUPS):
        lo, hi = g * FNET_GROUP, (g + 1) * FNET_GROUP
        xc = _dot(a[:, lo:hi], fc_ref[...])
        xf_ref[:, lo:hi] = xc[:, :FNET_GROUP].astype(BF16)
        xf_ref[:, BRANCH + lo:BRANCH + hi] = xc[:, FNET_GROUP:].astype(BF16)

    c = _dot(h, w_ref[:, 3 * BRANCH:4 * BRANCH])
    for q in range(S5_Q):
        c4_ref[q] = c[:, q * FNET_GROUP:(q + 1) * FNET_GROUP]

    u = jax.nn.gelu(_dot(h, w_ref[:, BRANCH:2 * BRANCH]))
    v = jax.nn.gelu(_dot(h, w_ref[:, 2 * BRANCH:3 * BRANCH]))
    v = (_rms(v, vg_ref[...])).astype(BF16)
    lane = lax.broadcasted_iota(jnp.int32, (SGU_CHUNK, FNET_GROUP), 1)
    first_head = lane < SGU_HEAD_DIM
    zero = jnp.zeros((SGU_CHUNK, FNET_GROUP), BF16)
    for ck in range(tm // SGU_CHUNK):
        rows = slice(ck * SGU_CHUNK, (ck + 1) * SGU_CHUNK)
        for j in range(SGU_HEADS // 2):
            cols = slice(j * FNET_GROUP, (j + 1) * FNET_GROUP)
            blk = v[rows, cols]
            rhs = jnp.concatenate([jnp.where(first_head, blk, zero),
                                   jnp.where(first_head, zero, blk)], axis=0)
            mixed = _dot(ws_ref[j], rhs) + sb_ref[:, cols]
            sg_ref[rows, cols] = (u[rows, cols] * mixed).astype(BF16)


def _proj(x, g, w, fc, vg, ws, sb):
    t = x.shape[0]
    tm = TOKEN_TILE
    return pl.pallas_call(
        functools.partial(_proj_kernel, tm=tm),
        grid=(t // tm,),
        in_specs=[
            pl.BlockSpec((tm, D_MODEL), lambda i: (i, 0)),
            _const_spec((1, D_MODEL)),
            _const_spec((D_MODEL, 4 * BRANCH)),
            _const_spec((FNET_GROUP, 2 * FNET_GROUP)),
            _const_spec((1, BRANCH)),
            _const_spec((SGU_HEADS // 2, SGU_CHUNK, 2 * SGU_CHUNK)),
            _const_spec((SGU_CHUNK, BRANCH)),
        ],
        out_specs=[
            pl.BlockSpec((tm, 2 * BRANCH), lambda i: (i, 0)),
            pl.BlockSpec((tm, BRANCH), lambda i: (i, 0)),
            pl.BlockSpec((S5_Q, tm, FNET_GROUP), lambda i: (0, i, 0)),
        ],
        out_shape=[
            jax.ShapeDtypeStruct((t, 2 * BRANCH), BF16),
            jax.ShapeDtypeStruct((t, BRANCH), BF16),
            jax.ShapeDtypeStruct((S5_Q, t, FNET_GROUP), F32),
        ],
        compiler_params=_cparams("parallel"),
        name="proj",
    )(x, g, w, fc, vg, ws, sb)


def _fft1_kernel(x_ref, f_ref, z_ref, *, n1, sb):
    for j in range(sb):
        o = j * 2 * BRANCH
        rhs = jnp.concatenate([x_ref[0, :, o:o + BRANCH],
                               x_ref[0, :, o + BRANCH:o + 2 * BRANCH]], axis=0)
        z = _dot(f_ref[j], rhs)
        z_ref[0, :, o:o + BRANCH] = z[:n1].astype(BF16)
        z_ref[0, :, o + BRANCH:o + 2 * BRANCH] = z[n1:].astype(BF16)


def _fft1(xf, f1, b, n1):
    sb = 8
    width = FFT_N2 * 2 * BRANCH
    return pl.pallas_call(
        functools.partial(_fft1_kernel, n1=n1, sb=sb),
        grid=(b, FFT_N2 // sb),
        in_specs=[
            pl.BlockSpec((1, n1, sb * 2 * BRANCH), lambda i, j: (i, 0, j)),
            pl.BlockSpec((sb, 2 * n1, 2 * n1), lambda i, j: (j, 0, 0)),
        ],
        out_specs=pl.BlockSpec((1, n1, sb * 2 * BRANCH), lambda i, j: (i, 0, j)),
        out_shape=jax.ShapeDtypeStruct((b, n1, width), BF16),
        compiler_params=_cparams("parallel", "parallel"),
        name="fft1",
    )(xf.reshape(b, n1, width), f1)


def _fft2_kernel(z_ref, f_ref, o_ref, *, kb):
    for k in range(kb):
        rows = slice(k * FFT_N2, (k + 1) * FFT_N2)
        rhs = jnp.concatenate([z_ref[0, rows, 0:BRANCH],
                               z_ref[0, rows, BRANCH:2 * BRANCH]], axis=0)
        o_ref[0, k] = _dot(f_ref[...], rhs).astype(BF16)


def _fft2(z, f2, b, n1):
    kb = 8
    return pl.pallas_call(
        functools.partial(_fft2_kernel, kb=kb),
        grid=(b, n1 // kb),
        in_specs=[
            pl.BlockSpec((1, kb * FFT_N2, 2 * BRANCH), lambda i, j: (i, j, 0)),
            _const_spec((FFT_N2, 2 * FFT_N2)),
        ],
        out_specs=pl.BlockSpec((1, kb, FFT_N2, BRANCH), lambda i, j: (i, j, 0, 0)),
        out_shape=jax.ShapeDtypeStruct((b, n1, FFT_N2, BRANCH), BF16),
        compiler_params=_cparams("parallel", "parallel"),
        name="fft2",
    )(z.reshape(b, n1 * FFT_N2, 2 * BRANCH), f2)


def _s5a_kernel(x_ref, m_ref, e_ref):
    e_ref[0] = _dot(x_ref[0].astype(BF16), m_ref[0])


def _s5a(c4, mend, rows):
    r = min(512, rows)
    return pl.pallas_call(
        _s5a_kernel,
        grid=(S5_Q, rows // r),
        in_specs=[
            pl.BlockSpec((1, r, S5_W), lambda q, i: (q, i, 0)),
            pl.BlockSpec((1, S5_W, S5_W), lambda q, i: (q, 0, 0)),
        ],
        out_specs=pl.BlockSpec((1, r, S5_W), lambda q, i: (q, i, 0)),
        out_shape=jax.ShapeDtypeStruct((S5_Q, rows, S5_W), F32),
        compiler_params=_cparams("parallel", "parallel"),
        name="s5a",
    )(c4, mend)


def _s5b_kernel(a_ref, e_ref, h_ref, *, nc):
    backward = pl.program_id(2) == 1
    half = S5_W // 4
    ar = a_ref[0, 0, :, 0:half]
    ai = a_ref[0, 0, :, half:2 * half]

    def body(n, carry):
        hr, hi = carry
        idx = jnp.where(backward, nc - 1 - n, n)
        h_ref[0, 0, pl.ds(idx, 1), 0:half] = hr
        h_ref[0, 0, pl.ds(idx, 1), half:2 * half] = hi
        er = e_ref[0, 0, pl.ds(idx, 1), 0:half]
        ei = e_ref[0, 0, pl.ds(idx, 1), half:2 * half]
        return ar * hr - ai * hi + er, ar * hi + ai * hr + ei

    zero = jnp.zeros((1, half), F32)
    lax.fori_loop(0, nc, body, (zero, zero))


def _s5b(e4, decay, b, nc):
    half_w = S5_W // 2
    return pl.pallas_call(
        functools.partial(_s5b_kernel, nc=nc),
        grid=(S5_Q, b, 2),
        in_specs=[
            pl.BlockSpec((1, 1, 1, half_w), lambda q, i, d: (q, d, 0, 0)),
            pl.BlockSpec((1, 1, nc, half_w), lambda q, i, d: (q, i, 0, d)),
        ],
        out_specs=pl.BlockSpec((1, 1, nc, half_w), lambda q, i, d: (q, i, 0, d)),
        out_shape=jax.ShapeDtypeStruct((S5_Q, b, nc, S5_W), F32),
        compiler_params=_cparams("parallel", "parallel", "parallel"),
        name="s5b",
    )(decay, e4.reshape(S5_Q, b, nc, S5_W))


def _s5c_kernel(x_ref, h_ref, t_ref, r_ref, d_ref, y_ref):
    x = x_ref[0]
    y = _dot(x.astype(BF16), t_ref[0]) + _dot(h_ref[0].astype(BF16), r_ref[0])
    y_ref[0] = y + x * d_ref[0]


def _s5c(c4, h4, toep, rout, dskip, rows):
    r = min(256, rows)
    return pl.pallas_call(
        _s5c_kernel,
        grid=(S5_Q, rows // r),
        in_specs=[
            pl.BlockSpec((1, r, S5_W), lambda q, i: (q, i, 0)),
            pl.BlockSpec((1, r, S5_W), lambda q, i: (q, i, 0)),
            pl.BlockSpec((1, S5_W, S5_W), lambda q, i: (q, 0, 0)),
            pl.BlockSpec((1, S5_W, S5_W), lambda q, i: (q, 0, 0)),
            pl.BlockSpec((1, 1, S5_W), lambda q, i: (q, 0, 0)),
        ],
        out_specs=pl.BlockSpec((1, r, S5_W), lambda q, i: (q, i, 0)),
        out_shape=jax.ShapeDtypeStruct((S5_Q, rows, S5_W), F32),
        compiler_params=_cparams("parallel", "parallel"),
        name="s5c",
    )(c4, h4, toep, rout, dskip)


def _merge_kernel(x_ref, fo_ref, sg_ref, y4_ref, g_ref, wg_ref, wf_ref, ws_ref,
                  wglu_ref, wo_ref, o_ref, m_ref, *, nsub):
    x = x_ref[...]
    h = _rms(x, g_ref[...]).astype(BF16)
    fo = jnp.concatenate([fo_ref[0, :, j * BRANCH:(j + 1) * BRANCH] for j in range(nsub)],
                         axis=0)
    sg = sg_ref[...]
    ys = jax.nn.gelu(jnp.concatenate([y4_ref[q] for q in range(S5_Q)], axis=1)).astype(BF16)
    nb = 256
    for c0 in range(0, D_MODEL, nb):
        cs = slice(c0, c0 + nb)
        ga = jax.nn.sigmoid(_dot(h, wg_ref[:, c0:c0 + nb]))
        gb = jax.nn.sigmoid(_dot(h, wg_ref[:, D_MODEL + c0:D_MODEL + c0 + nb]))
        gc = jax.nn.sigmoid(_dot(h, wg_ref[:, 2 * D_MODEL + c0:2 * D_MODEL + c0 + nb]))
        ya = _dot(fo, wf_ref[:, cs])
        yb = _dot(sg, ws_ref[:, cs])
        val = _dot(ys, wglu_ref[:, cs])
        gate = _dot(ys, wglu_ref[:, D_MODEL + c0:D_MODEL + c0 + nb])
        yc = val * jax.nn.sigmoid(gate)
        m_ref[:, cs] = (ga * ya + gb * yb + gc * yc).astype(BF16)
    o_ref[...] = x + _dot(m_ref[...], wo_ref[...])


def _merge(x, fo, sg, y4, g, wg, wf, ws, wglu, wo, b, n1):
    t = x.shape[0]
    tm = TOKEN_TILE
    nsub = tm // n1
    per_seq = FFT_N2 // nsub
    return pl.pallas_call(
        functools.partial(_merge_kernel, nsub=nsub),
        grid=(t // tm,),
        in_specs=[
            pl.BlockSpec((tm, D_MODEL), lambda i: (i, 0)),
            pl.BlockSpec((1, n1, nsub * BRANCH), lambda i: (i // per_seq, 0, i % per_seq)),
            pl.BlockSpec((tm, BRANCH), lambda i: (i, 0)),
            pl.BlockSpec((S5_Q, tm, FNET_GROUP), lambda i: (0, i, 0)),
            _const_spec((1, D_MODEL)),
            _const_spec((D_MODEL, 3 * D_MODEL)),
            _const_spec((BRANCH, D_MODEL)),
            _const_spec((BRANCH, D_MODEL)),
            _const_spec((BRANCH, 2 * D_MODEL)),
            _const_spec((D_MODEL, D_MODEL)),
        ],
        out_specs=pl.BlockSpec((tm, D_MODEL), lambda i: (i, 0)),
        out_shape=jax.ShapeDtypeStruct((t, D_MODEL), F32),
        scratch_shapes=[pltpu.VMEM((tm, D_MODEL), BF16)],
        compiler_params=_cparams("parallel"),
        name="merge",
    )(x, fo.reshape(b, n1, FFT_N2 * BRANCH), sg, y4.reshape(S5_Q, t, FNET_GROUP),
      g, wg, wf, ws, wglu, wo)


def _ffn_kernel(xp_ref, x_ref, xn_ref, g_ref, wup_ref, cw_ref, wdn_ref, gf_ref,
                o_ref, h_ref, acc_ref, *, tm, seq, final):
    pos = lax.rem(pl.program_id(0) * tm, seq)
    keep_prev = (pos != 0).astype(F32)
    keep_next = (pos + tm != seq).astype(F32)
    g = g_ref[...]
    h_ref[0:HALO] = (_rms(xp_ref[...], g) * keep_prev).astype(BF16)
    h_ref[HALO:HALO + tm] = _rms(x_ref[...], g).astype(BF16)
    h_ref[HALO + tm:] = (_rms(xn_ref[...], g) * keep_next).astype(BF16)
    hext = h_ref[...]
    acc_ref[...] = jnp.zeros_like(acc_ref)
    rows = tm + 2 * HALO

    def body(j, carry):
        up = _dot(hext, wup_ref[j])
        cw = cw_ref[j]
        conv = (pltpu.roll(up, 1, 0) * cw[0:1] + up * cw[1:2]
                + pltpu.roll(up, rows - 1, 0) * cw[2:3] + cw[3:4])
        conv = conv[HALO:HALO + tm]
        act = jax.nn.silu(conv[:, :FF_BLK]) * conv[:, FF_BLK:]
        acc_ref[...] += _dot(act.astype(BF16), wdn_ref[j])
        return carry

    lax.fori_loop(0, FF_NBLK, body, 0)
    y = x_ref[...] + acc_ref[...]
    if final:
        y = _rms(y, gf_ref[...])
    o_ref[...] = y


def _ffn(x, g, wup, cw, wdn, gf, seq, final):
    t = x.shape[0]
    tm = TOKEN_TILE
    per = tm // HALO
    last = t // HALO - 1
    return pl.pallas_call(
        functools.partial(_ffn_kernel, tm=tm, seq=seq, final=final),
        grid=(t // tm,),
        in_specs=[
            pl.BlockSpec((HALO, D_MODEL), lambda i: (jnp.maximum(i * per - 1, 0), 0)),
            pl.BlockSpec((tm, D_MODEL), lambda i: (i, 0)),
            pl.BlockSpec((HALO, D_MODEL), lambda i: (jnp.minimum((i + 1) * per, last), 0)),
            _const_spec((1, D_MODEL)),
            _const_spec((FF_NBLK, D_MODEL, 2 * FF_BLK)),
            _const_spec((FF_NBLK, 8, 2 * FF_BLK)),
            _const_spec((FF_NBLK, FF_BLK, D_MODEL)),
            _const_spec((1, D_MODEL)),
        ],
        out_specs=pl.BlockSpec((tm, D_MODEL), lambda i: (i, 0)),
        out_shape=jax.ShapeDtypeStruct((t, D_MODEL), F32),
        scratch_shapes=[pltpu.VMEM((tm + 2 * HALO, D_MODEL), BF16),
                        pltpu.VMEM((tm, D_MODEL), F32)],
        compiler_params=_cparams("parallel"),
        name="ffn_final" if final else "ffn",
    )(x, x, x, g, wup, cw, wdn, gf)


def _dft_consts(n1):
    n2 = FFT_N2
    s = n1 * n2
    k = np.arange(FNET_GROUP)
    ang = 2 * np.pi * ((k[:, None] * k[None, :]) % FNET_GROUP) / FNET_GROUP
    fc = np.concatenate([np.cos(ang), -np.sin(ang)], axis=1) / math.sqrt(FNET_GROUP)
    k1 = np.arange(n1)[None, :, None]
    s1 = np.arange(n1)[None, None, :]
    s2 = np.arange(n2)[:, None, None]
    ang = 2 * np.pi * ((k1 * (n2 * s1 + s2)) % s) / s
    c, sn = np.cos(ang), np.sin(ang)
    f1 = np.concatenate([np.concatenate([c, sn], axis=2),
                         np.concatenate([-sn, c], axis=2)], axis=1) / math.sqrt(n1)
    k2 = np.arange(n2)
    ang = 2 * np.pi * ((k2[:, None] * k2[None, :]) % n2) / n2
    f2 = np.concatenate([np.cos(ang), np.sin(ang)], axis=1) / math.sqrt(n2)
    return (jnp.asarray(fc, BF16), jnp.asarray(f1, BF16), jnp.asarray(f2, BF16))


def _cmul(ar, ai, br, bi):
    return ar * br - ai * bi, ar * bi + ai * br


def _s5_operators(lam_re, lam_im, log_dt, b_re, b_im, c_re, c_im, d_skip):
    L, G, P, H, Q, GQ = S5_L, S5_G, S5_P, S5_H, S5_Q, S5_GQ
    dt = jnp.exp(log_dt)[..., None]
    lr, li = lam_re * dt, lam_im * dt

    def lb_pow(n):
        n = jnp.asarray(n, F32)[:, None, None, None]
        mag = jnp.exp(n * lr)
        return mag * jnp.cos(n * li), mag * jnp.sin(n * li)

    lbr, lbi = lb_pow([1.0])
    den = lam_re * lam_re + lam_im * lam_im
    bsr, bsi = _cmul(lbr[0] - 1.0, lbi[0], lam_re / den, -lam_im / den)
    cbr, cbi = _cmul(c_re, c_im, bsr[:, :, None, :], bsi[:, :, None, :])

    pw_r, pw_i = lb_pow(np.arange(L + 1))

    def lag_kernel(d):
        wr, wi = _cmul(cbr[d][None], cbi[d][None],
                       pw_r[:L, d][:, :, None, :], pw_i[:L, d][:, :, None, :])
        return (jnp.einsum('sghp,gpk->sghk', wr, b_re)
                - jnp.einsum('sghp,gpk->sghk', wi, b_im))

    kf, kb = lag_kernel(0), lag_kernel(1)
    klag = jnp.concatenate([kb[1:][::-1], (kf[0] + kb[0])[None], kf[1:]], axis=0)
    idx = np.arange(L)[None, :] - np.arange(L)[:, None] + (L - 1)
    kss = klag[idx]
    kss = kss.reshape(L, L, Q, GQ, H, H)
    eye = jnp.eye(GQ, dtype=F32)
    toep = jnp.einsum('stqahk,ab->qsaktbh', kss, eye).reshape(Q, S5_W, S5_W)

    def end_state(d, exps):
        er, ei = pw_r[exps, d], pw_i[exps, d]
        vr, vi = _cmul(er[..., None], ei[..., None], b_re[None], b_im[None])
        vr = vr.reshape(L, Q, GQ, P, H)
        vi = vi.reshape(L, Q, GQ, P, H)
        mr = jnp.einsum('sqapk,ab->qsakbp', vr, eye)
        mi = jnp.einsum('sqapk,ab->qsakbp', vi, eye)
        return mr.reshape(Q, S5_W, GQ * P), mi.reshape(Q, S5_W, GQ * P)

    fr, fi = end_state(0, np.arange(L)[::-1])
    br_, bi_ = end_state(1, np.arange(L))
    mend = jnp.concatenate([fr, fi, br_, bi_], axis=2)

    def read_out(d, exps):
        wr, wi = _cmul(cbr[d][None], cbi[d][None],
                       pw_r[exps, d][:, :, None, :], pw_i[exps, d][:, :, None, :])
        wr = wr.reshape(L, Q, GQ, H, P)
        wi = wi.reshape(L, Q, GQ, H, P)
        rr = jnp.einsum('tqbhp,ab->qaptbh', wr, eye)
        ri = jnp.einsum('tqbhp,ab->qaptbh', -wi, eye)
        return rr.reshape(Q, GQ * P, S5_W), ri.reshape(Q, GQ * P, S5_W)

    rfr, rfi = read_out(0, np.arange(1, L + 1))
    rbr, rbi = read_out(1, np.arange(L, 0, -1))
    rout = jnp.concatenate([rfr, rfi, rbr, rbi], axis=1)

    ar = pw_r[L].reshape(2, Q, GQ * P)
    ai = pw_i[L].reshape(2, Q, GQ * P)
    decay = jnp.transpose(jnp.concatenate([ar, ai], axis=2), (1, 0, 2))[:, :, None, :]

    dtile = jnp.tile(d_skip.reshape(Q, 1, GQ * H), (1, L, 1)).reshape(Q, 1, S5_W)
    return mend.astype(BF16), decay, toep.astype(BF16), rout.astype(BF16), dtile


def _pack_sgu(sgu_w, sgu_b):
    ws = jnp.concatenate([sgu_w[0::2], sgu_w[1::2]], axis=2).astype(BF16)
    sb = jnp.repeat(sgu_b.T, SGU_HEAD_DIM, axis=1)
    return ws, sb


def _pack_ffn(w_up, conv_w, conv_b, w_down):
    def blocks(a):
        gate = a[..., :D_FF].reshape(a.shape[:-1] + (FF_NBLK, FF_BLK))
        val = a[..., D_FF:].reshape(a.shape[:-1] + (FF_NBLK, FF_BLK))
        both = jnp.concatenate([gate, val], axis=-1)
        return jnp.moveaxis(both, -2, 0)
    wup = blocks(w_up).astype(BF16)
    taps = jnp.concatenate([conv_w, conv_b[None], jnp.zeros((4, 2 * D_FF), F32)], axis=0)
    cw = blocks(taps)
    wdn = w_down.reshape(FF_NBLK, FF_BLK, D_MODEL).astype(BF16)
    return wup, cw, wdn


def _trunk(x, layers, g_final, consts):
    b, seq, _ = x.shape
    n1 = seq // FFT_N2
    nc = seq // S5_L
    t = b * seq
    fc, f1, f2 = consts
    x = x.reshape(t, D_MODEL)
    for li, p in enumerate(layers):
        xf, sg, c4 = _proj(x, p["g_mix"], p["w_in_a"], fc, p["v_gain"], p["sgu_w"], p["sgu_b"])
        fo = _fft2(_fft1(xf, f1, b, n1), f2, b, n1)
        c4 = c4.reshape(S5_Q, b * nc, S5_W)
        e4 = _s5a(c4, p["mend"], b * nc)
        h4 = _s5b(e4, p["decay"], b, nc).reshape(S5_Q, b * nc, S5_W)
        y4 = _s5c(c4, h4, p["toep"], p["rout"], p["dtile"], b * nc)
        x = _merge(x, fo, sg, y4, p["g_mix"], p["w_in_g"], p["w_fnet"], p["w_sgu"],
                   p["w_glu"], p["w_out"], b, n1)
        x = _ffn(x, p["g_ffn"], p["w_up"], p["conv"], p["w_down"], g_final, seq,
                 final=(li == len(layers) - 1))
    return x.reshape(b, seq, D_MODEL)


def kernel(x_prompt, x_sample, g_mix, w_in, w_fnet, sgu_v_gain, sgu_w, sgu_b, w_sgu,
           s5_lam_re, s5_lam_im, s5_log_dt, s5_b_re, s5_b_im, s5_c_re, s5_c_im, s5_d,
           w_glu, w_out, g_ffn, w_up, conv_w, conv_b, w_down, g_final):
    depth = w_in.shape[0]
    layers = []
    for i in range(depth):
        ws, sb = _pack_sgu(sgu_w[i], sgu_b[i])
        wup, cw, wdn = _pack_ffn(w_up[i], conv_w[i], conv_b[i], w_down[i])
        mend, decay, toep, rout, dtile = _s5_operators(
            s5_lam_re[i], s5_lam_im[i], s5_log_dt[i], s5_b_re[i], s5_b_im[i],
            s5_c_re[i], s5_c_im[i], s5_d[i])
        layers.append(dict(
            g_mix=g_mix[i][None], w_in_a=w_in[i][:, :4 * BRANCH].astype(BF16),
            w_in_g=w_in[i][:, 4 * BRANCH:].astype(BF16),
            v_gain=sgu_v_gain[i][None], sgu_w=ws, sgu_b=sb,
            w_fnet=w_fnet[i].astype(BF16), w_sgu=w_sgu[i].astype(BF16),
            w_glu=w_glu[i].astype(BF16), w_out=w_out[i].astype(BF16),
            mend=mend, decay=decay, toep=toep, rout=rout, dtile=dtile,
            g_ffn=g_ffn[i][None], w_up=wup, conv=cw, w_down=wdn))
    gf = g_final[None]
    outs = []
    for x in (x_prompt, x_sample):
        consts = _dft_consts(x.shape[1] // FFT_N2)
        outs.append(_trunk(x, layers, gf, consts))
    return tuple(outs)
```

```python
import functools
import math

import numpy as np
import jax
import jax.numpy as jnp
from jax import lax
from jax.experimental import pallas as pl
from jax.experimental.pallas import tpu as pltpu

F32 = jnp.float32
BF16 = jnp.bfloat16

D_MODEL = 1024
BRANCH = 512
FNET_GROUP = 128
FNET_GROUPS = BRANCH // FNET_GROUP
SGU_CHUNK = 128
SGU_HEADS = 8
SGU_HEAD_DIM = BRANCH // SGU_HEADS
S5_H = 16
S5_G = BRANCH // S5_H
S5_P = 64
S5_L = 16
S5_Q = 4
S5_GQ = S5_G // S5_Q
S5_W = S5_L * FNET_GROUP
S5_PIECES = S5_W // FNET_GROUP
D_FF = 2816
FF_BLK = 256
FF_NBLK = D_FF // FF_BLK
RMS_EPS = 1e-6
FFT_N2 = 128
HALO = 16

TOKEN_TILE = 512
VMEM_LIMIT = 56 * 1024 * 1024


def _cparams(*sem):
    return pltpu.CompilerParams(dimension_semantics=sem, vmem_limit_bytes=VMEM_LIMIT)


def _const_spec(shape):
    nd = len(shape)
    return pl.BlockSpec(shape, lambda *_: (0,) * nd)


def _dot(a, b):
    return jnp.dot(a, b, preferred_element_type=F32)


def _rms(x, g):
    return x * lax.rsqrt(jnp.mean(x * x, axis=-1, keepdims=True) + RMS_EPS) * g


def _pack_complex(re, im):
    hi = lax.bitcast_convert_type(re.astype(BF16).astype(F32), jnp.uint32)
    lo = lax.bitcast_convert_type(im.astype(BF16).astype(F32), jnp.uint32)
    return hi | (lo >> 16)


def _unpack_complex(w):
    re = lax.bitcast_convert_type(w & jnp.uint32(0xFFFF0000), F32)
    im = lax.bitcast_convert_type(w << 16, F32)
    return jnp.concatenate([re, im], axis=0).astype(BF16)


def _proj_kernel(x_ref, g_ref, w_ref, fc_ref, vg_ref, ws_ref, sb_ref,
                 xf_ref, sg_ref, c4_ref, c_scr, *, tm):
    h = _rms(x_ref[...], g_ref[...]).astype(BF16)

    a = _dot(h, w_ref[:, 0:BRANCH]).astype(BF16)
    for g in range(FNET_GROUPS):
        lo, hi = g * FNET_GROUP, (g + 1) * FNET_GROUP
        xc = _dot(a[:, lo:hi], fc_ref[...])
        xf_ref[:, lo:hi] = _pack_complex(xc[:, :FNET_GROUP], xc[:, FNET_GROUP:])

    c = _dot(h, w_ref[:, 3 * BRANCH:4 * BRANCH])
    for q in range(S5_Q):
        c_scr[q] = c[:, q * FNET_GROUP:(q + 1) * FNET_GROUP]
    for q in range(S5_Q):
        for s in range(S5_L):
            c4_ref[q, :, s * FNET_GROUP:(s + 1) * FNET_GROUP] = (
                c_scr[q, pl.ds(s, tm // S5_L, stride=S5_L), :])

    u = jax.nn.gelu(_dot(h, w_ref[:, BRANCH:2 * BRANCH]))
    v = jax.nn.gelu(_dot(h, w_ref[:, 2 * BRANCH:3 * BRANCH]))
    v = (_rms(v, vg_ref[...])).astype(BF16)
    lane = lax.broadcasted_iota(jnp.int32, (SGU_CHUNK, FNET_GROUP), 1)
    first_head = lane < SGU_HEAD_DIM
    zero = jnp.zeros((SGU_CHUNK, FNET_GROUP), BF16)
    for ck in range(tm // SGU_CHUNK):
        rows = slice(ck * SGU_CHUNK, (ck + 1) * SGU_CHUNK)
        for j in range(SGU_HEADS // 2):
            cols = slice(j * FNET_GROUP, (j + 1) * FNET_GROUP)
            blk = v[rows, cols]
            rhs = jnp.concatenate([jnp.where(first_head, blk, zero),
                                   jnp.where(first_head, zero, blk)], axis=0)
            mixed = _dot(ws_ref[j], rhs) + sb_ref[:, cols]
            sg_ref[rows, cols] = (u[rows, cols] * mixed).astype(BF16)


def _proj(x, g, w, fc, vg, ws, sb):
    t = x.shape[0]
    tm = TOKEN_TILE
    return pl.pallas_call(
        functools.partial(_proj_kernel, tm=tm),
        grid=(t // tm,),
        in_specs=[
            pl.BlockSpec((tm, D_MODEL), lambda i: (i, 0)),
            _const_spec((1, D_MODEL)),
            _const_spec((D_MODEL, 4 * BRANCH)),
            _const_spec((FNET_GROUP, 2 * FNET_GROUP)),
            _const_spec((1, BRANCH)),
            _const_spec((SGU_HEADS // 2, SGU_CHUNK, 2 * SGU_CHUNK)),
            _const_spec((SGU_CHUNK, BRANCH)),
        ],
        out_specs=[
            pl.BlockSpec((tm, BRANCH), lambda i: (i, 0)),
            pl.BlockSpec((tm, BRANCH), lambda i: (i, 0)),
            pl.BlockSpec((S5_Q, tm // S5_L, S5_W), lambda i: (0, i, 0)),
        ],
        out_shape=[
            jax.ShapeDtypeStruct((t, BRANCH), jnp.uint32),
            jax.ShapeDtypeStruct((t, BRANCH), BF16),
            jax.ShapeDtypeStruct((S5_Q, t // S5_L, S5_W), F32),
        ],
        scratch_shapes=[pltpu.VMEM((S5_Q, tm, FNET_GROUP), F32)],
        compiler_params=_cparams("parallel"),
        name="proj",
    )(x, g, w, fc, vg, ws, sb)


def _fft1_kernel(x_ref, f_ref, z_ref, *, n1, sb):
    for j in range(sb):
        z = _dot(f_ref[j], _unpack_complex(x_ref[0, :, j, :]))
        z_ref[0, :, j, :] = _pack_complex(z[:n1], z[n1:])


def _fft1(xf, f1, b, n1):
    sb = 8
    return pl.pallas_call(
        functools.partial(_fft1_kernel, n1=n1, sb=sb),
        grid=(b, FFT_N2 // sb),
        in_specs=[
            pl.BlockSpec((1, n1, sb, BRANCH), lambda i, j: (i, 0, j, 0)),
            pl.BlockSpec((sb, 2 * n1, 2 * n1), lambda i, j: (j, 0, 0)),
        ],
        out_specs=pl.BlockSpec((1, n1, sb, BRANCH), lambda i, j: (i, 0, j, 0)),
        out_shape=jax.ShapeDtypeStruct((b, n1, FFT_N2, BRANCH), jnp.uint32),
        compiler_params=_cparams("parallel", "parallel"),
        name="fft1",
    )(xf.reshape(b, n1, FFT_N2, BRANCH), f1)


def _fft2_kernel(z_ref, f_ref, o_ref, *, kb):
    for k in range(kb):
        o_ref[0, :, k, :] = _dot(f_ref[...], _unpack_complex(z_ref[0, k]))


def _fft2(z, f2, b, n1):
    kb = 8
    return pl.pallas_call(
        functools.partial(_fft2_kernel, kb=kb),
        grid=(b, n1 // kb),
        in_specs=[
            pl.BlockSpec((1, kb, FFT_N2, BRANCH), lambda i, j: (i, j, 0, 0)),
            _const_spec((FFT_N2, 2 * FFT_N2)),
        ],
        out_specs=pl.BlockSpec((1, FFT_N2, kb, BRANCH), lambda i, j: (i, 0, j, 0)),
        out_shape=jax.ShapeDtypeStruct((b, FFT_N2, n1, BRANCH), F32),
        compiler_params=_cparams("parallel", "parallel"),
        name="fft2",
    )(z, f2)


def _s5a_kernel(x_ref, m_ref, e_ref):
    e = _dot(x_ref[0].astype(BF16), m_ref[0])
    for p in range(S5_PIECES):
        e_ref[0, :, p, :] = e[:, p * FNET_GROUP:(p + 1) * FNET_GROUP]


def _s5a(c4, mend, rows):
    r = min(512, rows)
    return pl.pallas_call(
        _s5a_kernel,
        grid=(S5_Q, rows // r),
        in_specs=[
            pl.BlockSpec((1, r, S5_W), lambda q, i: (q, i, 0)),
            pl.BlockSpec((1, S5_W, S5_W), lambda q, i: (q, 0, 0)),
        ],
        out_specs=pl.BlockSpec((1, r, S5_PIECES, FNET_GROUP), lambda q, i: (q, i, 0, 0)),
        out_shape=jax.ShapeDtypeStruct((S5_Q, rows, S5_PIECES, FNET_GROUP), F32),
        compiler_params=_cparams("parallel", "parallel"),
        name="s5a",
    )(c4, mend)


def _s5b_kernel(a_ref, e_ref, h_ref, *, nc):
    arf, aif, arb, aib = a_ref[0, 0], a_ref[0, 1], a_ref[0, 2], a_ref[0, 3]
    half = S5_PIECES // 2

    def body(n, carry):
        hf, hb = carry
        m = nc - 1 - n
        h_ref[0, n, 0:half, :] = hf
        h_ref[0, m, half:S5_PIECES, :] = hb
        hf = arf * hf + aif * pltpu.roll(hf, half // 2, 0) + e_ref[0, n, 0:half, :]
        hb = arb * hb + aib * pltpu.roll(hb, half // 2, 0) + e_ref[0, m, half:S5_PIECES, :]
        return hf, hb

    zero = jnp.zeros((half, FNET_GROUP), F32)
    lax.fori_loop(0, nc, body, (zero, zero), unroll=4)


def _s5b(e4, decay, b, nc):
    blk = (1, nc, S5_PIECES, FNET_GROUP)
    return pl.pallas_call(
        functools.partial(_s5b_kernel, nc=nc),
        grid=(S5_Q, b),
        in_specs=[
            pl.BlockSpec((1, 4, S5_PIECES // 2, FNET_GROUP), lambda q, i: (q, 0, 0, 0)),
            pl.BlockSpec(blk, lambda q, i: (q * b + i, 0, 0, 0)),
        ],
        out_specs=pl.BlockSpec(blk, lambda q, i: (q * b + i, 0, 0, 0)),
        out_shape=jax.ShapeDtypeStruct((S5_Q * b, nc, S5_PIECES, FNET_GROUP), F32),
        compiler_params=_cparams("parallel", "parallel"),
        name="s5b",
    )(decay, e4.reshape(S5_Q * b, nc, S5_PIECES, FNET_GROUP))


def _s5c_kernel(x_ref, h_ref, t_ref, r_ref, d_ref, y_ref):
    x = x_ref[0]
    h = jnp.concatenate([h_ref[0, :, p, :] for p in range(S5_PIECES)], axis=1)
    y = _dot(x.astype(BF16), t_ref[0]) + _dot(h.astype(BF16), r_ref[0])
    y_ref[0] = y + x * d_ref[0]


def _s5c(c4, h4, toep, rout, dskip, rows):
    r = min(256, rows)
    return pl.pallas_call(
        _s5c_kernel,
        grid=(S5_Q, rows // r),
        in_specs=[
            pl.BlockSpec((1, r, S5_W), lambda q, i: (q, i, 0)),
            pl.BlockSpec((1, r, S5_PIECES, FNET_GROUP), lambda q, i: (q, i, 0, 0)),
            pl.BlockSpec((1, S5_W, S5_W), lambda q, i: (q, 0, 0)),
            pl.BlockSpec((1, S5_W, S5_W), lambda q, i: (q, 0, 0)),
            pl.BlockSpec((1, 1, S5_W), lambda q, i: (q, 0, 0)),
        ],
        out_specs=pl.BlockSpec((1, r, S5_W), lambda q, i: (q, i, 0)),
        out_shape=jax.ShapeDtypeStruct((S5_Q, rows, S5_W), F32),
        compiler_params=_cparams("parallel", "parallel"),
        name="s5c",
    )(c4, h4, toep, rout, dskip)


def _merge_kernel(x_ref, fo_ref, sg_ref, y4_ref, g_ref, wg_ref, wf_ref, ws_ref,
                  wglu_ref, wo_ref, o_ref, m_ref, y_scr, *, tm):
    x = x_ref[...]
    h = _rms(x, g_ref[...]).astype(BF16)
    fo = fo_ref[...].astype(BF16)
    sg = sg_ref[...]
    for q in range(S5_Q):
        for s in range(S5_L):
            y_scr[q, pl.ds(s, tm // S5_L, stride=S5_L), :] = (
                y4_ref[q, :, s * FNET_GROUP:(s + 1) * FNET_GROUP])
    ys = jax.nn.gelu(jnp.concatenate([y_scr[q] for q in range(S5_Q)], axis=1)).astype(BF16)
    nb = 256
    for c0 in range(0, D_MODEL, nb):
        cs = slice(c0, c0 + nb)
        ga = jax.nn.sigmoid(_dot(h, wg_ref[:, c0:c0 + nb]))
        gb = jax.nn.sigmoid(_dot(h, wg_ref[:, D_MODEL + c0:D_MODEL + c0 + nb]))
        gc = jax.nn.sigmoid(_dot(h, wg_ref[:, 2 * D_MODEL + c0:2 * D_MODEL + c0 + nb]))
        ya = _dot(fo, wf_ref[:, cs])
        yb = _dot(sg, ws_ref[:, cs])
        val = _dot(ys, wglu_ref[:, cs])
        gate = _dot(ys, wglu_ref[:, D_MODEL + c0:D_MODEL + c0 + nb])
        yc = val * jax.nn.sigmoid(gate)
        m_ref[:, cs] = (ga * ya + gb * yb + gc * yc).astype(BF16)
    o_ref[...] = x + _dot(m_ref[...], wo_ref[...])


def _merge(x, fo, sg, y4, g, wg, wf, ws, wglu, wo):
    t = x.shape[0]
    tm = TOKEN_TILE
    return pl.pallas_call(
        functools.partial(_merge_kernel, tm=tm),
        grid=(t // tm,),
        in_specs=[
            pl.BlockSpec((tm, D_MODEL), lambda i: (i, 0)),
            pl.BlockSpec((tm, BRANCH), lambda i: (i, 0)),
            pl.BlockSpec((tm, BRANCH), lambda i: (i, 0)),
            pl.BlockSpec((S5_Q, tm // S5_L, S5_W), lambda i: (0, i, 0)),
            _const_spec((1, D_MODEL)),
            _const_spec((D_MODEL, 3 * D_MODEL)),
            _const_spec((BRANCH, D_MODEL)),
            _const_spec((BRANCH, D_MODEL)),
            _const_spec((BRANCH, 2 * D_MODEL)),
            _const_spec((D_MODEL, D_MODEL)),
        ],
        out_specs=pl.BlockSpec((tm, D_MODEL), lambda i: (i, 0)),
        out_shape=jax.ShapeDtypeStruct((t, D_MODEL), F32),
        scratch_shapes=[pltpu.VMEM((tm, D_MODEL), BF16),
                        pltpu.VMEM((S5_Q, tm, FNET_GROUP), F32)],
        compiler_params=_cparams("parallel"),
        name="merge",
    )(x, fo, sg, y4, g, wg, wf, ws, wglu, wo)


def _ffn_kernel(xp_ref, x_ref, xn_ref, g_ref, wup_ref, cw_ref, wdn_ref, gf_ref,
                o_ref, h_ref, acc_ref, *, tm, seq, final):
    pos = lax.rem(pl.program_id(0) * tm, seq)
    keep_prev = (pos != 0).astype(F32)
    keep_next = (pos + tm != seq).astype(F32)
    g = g_ref[...]
    h_ref[0:HALO] = (_rms(xp_ref[...], g) * keep_prev).astype(BF16)
    h_ref[HALO:HALO + tm] = _rms(x_ref[...], g).astype(BF16)
    h_ref[HALO + tm:] = (_rms(xn_ref[...], g) * keep_next).astype(BF16)
    hext = h_ref[...]
    acc_ref[...] = jnp.zeros_like(acc_ref)
    rows = tm + 2 * HALO

    def body(j, carry):
        up = _dot(hext, wup_ref[j])
        cw = cw_ref[j]
        conv = (pltpu.roll(up, 1, 0) * cw[0:1] + up * cw[1:2]
                + pltpu.roll(up, rows - 1, 0) * cw[2:3] + cw[3:4])
        conv = conv[HALO:HALO + tm]
        act = jax.nn.silu(conv[:, :FF_BLK]) * conv[:, FF_BLK:]
        acc_ref[...] += _dot(act.astype(BF16), wdn_ref[j])
        return carry

    lax.fori_loop(0, FF_NBLK, body, 0, unroll=True)
    y = x_ref[...] + acc_ref[...]
    if final:
        y = _rms(y, gf_ref[...])
    o_ref[...] = y


def _ffn(x, g, wup, cw, wdn, gf, seq, final):
    t = x.shape[0]
    tm = TOKEN_TILE
    per = tm // HALO
    last = t // HALO - 1
    return pl.pallas_call(
        functools.partial(_ffn_kernel, tm=tm, seq=seq, final=final),
        grid=(t // tm,),
        in_specs=[
            pl.BlockSpec((HALO, D_MODEL), lambda i: (jnp.maximum(i * per - 1, 0), 0)),
            pl.BlockSpec((tm, D_MODEL), lambda i: (i, 0)),
            pl.BlockSpec((HALO, D_MODEL), lambda i: (jnp.minimum((i + 1) * per, last), 0)),
            _const_spec((1, D_MODEL)),
            _const_spec((FF_NBLK, D_MODEL, 2 * FF_BLK)),
            _const_spec((FF_NBLK, 8, 2 * FF_BLK)),
            _const_spec((FF_NBLK, FF_BLK, D_MODEL)),
            _const_spec((1, D_MODEL)),
        ],
        out_specs=pl.BlockSpec((tm, D_MODEL), lambda i: (i, 0)),
        out_shape=jax.ShapeDtypeStruct((t, D_MODEL), F32),
        scratch_shapes=[pltpu.VMEM((tm + 2 * HALO, D_MODEL), BF16),
                        pltpu.VMEM((tm, D_MODEL), F32)],
        compiler_params=_cparams("parallel"),
        name="ffn_final" if final else "ffn",
    )(x, x, x, g, wup, cw, wdn, gf)


def _dft_consts(n1):
    n2 = FFT_N2
    s = n1 * n2
    k = np.arange(FNET_GROUP)
    ang = 2 * np.pi * ((k[:, None] * k[None, :]) % FNET_GROUP) / FNET_GROUP
    fc = np.concatenate([np.cos(ang), -np.sin(ang)], axis=1) / math.sqrt(FNET_GROUP)
    k1 = np.arange(n1)[None, :, None]
    s1 = np.arange(n1)[None, None, :]
    s2 = np.arange(n2)[:, None, None]
    ang = 2 * np.pi * ((k1 * (n2 * s1 + s2)) % s) / s
    c, sn = np.cos(ang), np.sin(ang)
    f1 = np.concatenate([np.concatenate([c, sn], axis=2),
                         np.concatenate([-sn, c], axis=2)], axis=1) / math.sqrt(n1)
    k2 = np.arange(n2)
    ang = 2 * np.pi * ((k2[:, None] * k2[None, :]) % n2) / n2
    f2 = np.concatenate([np.cos(ang), np.sin(ang)], axis=1) / math.sqrt(n2)
    return (jnp.asarray(fc, BF16), jnp.asarray(f1, BF16), jnp.asarray(f2, BF16))


def _cmul(ar, ai, br, bi):
    return ar * br - ai * bi, ar * bi + ai * br


def _s5_operators(lam_re, lam_im, log_dt, b_re, b_im, c_re, c_im, d_skip):
    L, G, P, H, Q, GQ = S5_L, S5_G, S5_P, S5_H, S5_Q, S5_GQ
    dt = jnp.exp(log_dt)[..., None]
    lr, li = lam_re * dt, lam_im * dt

    def lb_pow(n):
        n = jnp.asarray(n, F32)[:, None, None, None]
        mag = jnp.exp(n * lr)
        return mag * jnp.cos(n * li), mag * jnp.sin(n * li)

    lbr, lbi = lb_pow([1.0])
    den = lam_re * lam_re + lam_im * lam_im
    bsr, bsi = _cmul(lbr[0] - 1.0, lbi[0], lam_re / den, -lam_im / den)
    cbr, cbi = _cmul(c_re, c_im, bsr[:, :, None, :], bsi[:, :, None, :])

    pw_r, pw_i = lb_pow(np.arange(L + 1))

    def lag_kernel(d):
        wr, wi = _cmul(cbr[d][None], cbi[d][None],
                       pw_r[:L, d][:, :, None, :], pw_i[:L, d][:, :, None, :])
        return (jnp.einsum('sghp,gpk->sghk', wr, b_re)
                - jnp.einsum('sghp,gpk->sghk', wi, b_im))

    kf, kb = lag_kernel(0), lag_kernel(1)
    klag = jnp.concatenate([kb[1:][::-1], (kf[0] + kb[0])[None], kf[1:]], axis=0)
    idx = np.arange(L)[None, :] - np.arange(L)[:, None] + (L - 1)
    kss = klag[idx]
    kss = kss.reshape(L, L, Q, GQ, H, H)
    eye = jnp.eye(GQ, dtype=F32)
    toep = jnp.einsum('stqahk,ab->qsaktbh', kss, eye).reshape(Q, S5_W, S5_W)

    def end_state(d, exps):
        er, ei = pw_r[exps, d], pw_i[exps, d]
        vr, vi = _cmul(er[..., None], ei[..., None], b_re[None], b_im[None])
        vr = vr.reshape(L, Q, GQ, P, H)
        vi = vi.reshape(L, Q, GQ, P, H)
        mr = jnp.einsum('sqapk,ab->qsakbp', vr, eye)
        mi = jnp.einsum('sqapk,ab->qsakbp', vi, eye)
        return mr.reshape(Q, S5_W, GQ * P), mi.reshape(Q, S5_W, GQ * P)

    fr, fi = end_state(0, np.arange(L)[::-1])
    br_, bi_ = end_state(1, np.arange(L))
    mend = jnp.concatenate([fr, fi, br_, bi_], axis=2)

    def read_out(d, exps):
        wr, wi = _cmul(cbr[d][None], cbi[d][None],
                       pw_r[exps, d][:, :, None, :], pw_i[exps, d][:, :, None, :])
        wr = wr.reshape(L, Q, GQ, H, P)
        wi = wi.reshape(L, Q, GQ, H, P)
        rr = jnp.einsum('tqbhp,ab->qaptbh', wr, eye)
        ri = jnp.einsum('tqbhp,ab->qaptbh', -wi, eye)
        return rr.reshape(Q, GQ * P, S5_W), ri.reshape(Q, GQ * P, S5_W)

    rfr, rfi = read_out(0, np.arange(1, L + 1))
    rbr, rbi = read_out(1, np.arange(L, 0, -1))
    rout = jnp.concatenate([rfr, rfi, rbr, rbi], axis=1)

    ar = pw_r[L].reshape(2, Q, GQ * P // FNET_GROUP, FNET_GROUP)
    ai = pw_i[L].reshape(2, Q, GQ * P // FNET_GROUP, FNET_GROUP)
    a_re = jnp.concatenate([ar, ar], axis=2)
    a_im = jnp.concatenate([-ai, ai], axis=2)
    decay = jnp.stack([a_re[0], a_im[0], a_re[1], a_im[1]], axis=1)

    dtile = jnp.tile(d_skip.reshape(Q, 1, GQ * H), (1, L, 1)).reshape(Q, 1, S5_W)
    return mend.astype(BF16), decay, toep.astype(BF16), rout.astype(BF16), dtile


def _pack_sgu(sgu_w, sgu_b):
    ws = jnp.concatenate([sgu_w[0::2], sgu_w[1::2]], axis=2).astype(BF16)
    sb = jnp.repeat(sgu_b.T, SGU_HEAD_DIM, axis=1)
    return ws, sb


def _pack_ffn(w_up, conv_w, conv_b, w_down):
    def blocks(a):
        gate = a[..., :D_FF].reshape(a.shape[:-1] + (FF_NBLK, FF_BLK))
        val = a[..., D_FF:].reshape(a.shape[:-1] + (FF_NBLK, FF_BLK))
        both = jnp.concatenate([gate, val], axis=-1)
        return jnp.moveaxis(both, -2, 0)
    wup = blocks(w_up).astype(BF16)
    taps = jnp.concatenate([conv_w, conv_b[None], jnp.zeros((4, 2 * D_FF), F32)], axis=0)
    cw = blocks(taps)
    wdn = w_down.reshape(FF_NBLK, FF_BLK, D_MODEL).astype(BF16)
    return wup, cw, wdn


def _trunk(x, layers, g_final, consts):
    b, seq, _ = x.shape
    n1 = seq // FFT_N2
    nc = seq // S5_L
    t = b * seq
    fc, f1, f2 = consts
    x = x.reshape(t, D_MODEL)
    for li, p in enumerate(layers):
        xf, sg, c4 = _proj(x, p["g_mix"], p["w_in_a"], fc, p["v_gain"], p["sgu_w"], p["sgu_b"])
        fo = _fft2(_fft1(xf, f1, b, n1), f2, b, n1).reshape(t, BRANCH)
        e4 = _s5a(c4, p["mend"], b * nc)
        h4 = _s5b(e4, p["decay"], b, nc).reshape(S5_Q, b * nc, S5_PIECES, FNET_GROUP)
        y4 = _s5c(c4, h4, p["toep"], p["rout"], p["dtile"], b * nc)
        x = _merge(x, fo, sg, y4, p["g_mix"], p["w_in_g"], p["w_fnet"], p["w_sgu"],
                   p["w_glu"], p["w_out"])
        x = _ffn(x, p["g_ffn"], p["w_up"], p["conv"], p["w_down"], g_final, seq,
                 final=(li == len(layers) - 1))
    return x.reshape(b, seq, D_MODEL)


def kernel(x_prompt, x_sample, g_mix, w_in, w_fnet, sgu_v_gain, sgu_w, sgu_b, w_sgu,
           s5_lam_re, s5_lam_im, s5_log_dt, s5_b_re, s5_b_im, s5_c_re, s5_c_im, s5_d,
           w_glu, w_out, g_ffn, w_up, conv_w, conv_b, w_down, g_final):
    depth = w_in.shape[0]
    layers = []
    for i in range(depth):
        ws, sb = _pack_sgu(sgu_w[i], sgu_b[i])
        wup, cw, wdn = _pack_ffn(w_up[i], conv_w[i], conv_b[i], w_down[i])
        mend, decay, toep, rout, dtile = _s5_operators(
            s5_lam_re[i], s5_lam_im[i], s5_log_dt[i], s5_b_re[i], s5_b_im[i],
            s5_c_re[i], s5_c_im[i], s5_d[i])
        layers.append(dict(
            g_mix=g_mix[i][None], w_in_a=w_in[i][:, :4 * BRANCH].astype(BF16),
            w_in_g=w_in[i][:, 4 * BRANCH:].astype(BF16),
            v_gain=sgu_v_gain[i][None], sgu_w=ws, sgu_b=sb,
            w_fnet=w_fnet[i].astype(BF16), w_sgu=w_sgu[i].astype(BF16),
            w_glu=w_glu[i].astype(BF16), w_out=w_out[i].astype(BF16),
            mend=mend, decay=decay, toep=toep, rout=rout, dtile=dtile,
            g_ffn=g_ffn[i][None], w_up=wup, conv=cw, w_down=wdn))
    gf = g_final[None]
    outs = []
    for x in (x_prompt, x_sample):
        consts = _dft_consts(x.shape[1] // FFT_N2)
        outs.append(_trunk(x, layers, gf, consts))
    return tuple(outs)
```

```python
import functools
import math

import numpy as np
import jax
import jax.numpy as jnp
from jax import lax
from jax.experimental import pallas as pl
from jax.experimental.pallas import tpu as pltpu

F32 = jnp.float32
BF16 = jnp.bfloat16

D_MODEL = 1024
BRANCH = 512
FNET_GROUP = 128
FNET_GROUPS = BRANCH // FNET_GROUP
SGU_CHUNK = 128
SGU_HEADS = 8
SGU_HEAD_DIM = BRANCH // SGU_HEADS
S5_H = 16
S5_G = BRANCH // S5_H
S5_P = 64
S5_L = 16
S5_Q = 4
S5_GQ = S5_G // S5_Q
S5_W = S5_L * FNET_GROUP
S5_PIECES = S5_W // FNET_GROUP
S5_PAIRS = S5_GQ // 2
S5_PW = 2 * S5_L * S5_H
D_FF = 2816
FF_BLK = 256
FF_NBLK = D_FF // FF_BLK
RMS_EPS = 1e-6
FFT_N2 = 128
HALO = 16

TOKEN_TILE = 512
FFN_TILE = 1024
VMEM_LIMIT = 56 * 1024 * 1024


def _cparams(*sem):
    return pltpu.CompilerParams(dimension_semantics=sem, vmem_limit_bytes=VMEM_LIMIT)


def _const_spec(shape):
    nd = len(shape)
    return pl.BlockSpec(shape, lambda *_: (0,) * nd)


def _resident_spec(shape):
    nd = len(shape)
    return pl.BlockSpec(shape, lambda *_: (0,) * nd, pipeline_mode=pl.Buffered(1))


def _dot(a, b):
    return jnp.dot(a, b, preferred_element_type=F32)


def _rms(x, g):
    return x * lax.rsqrt(jnp.mean(x * x, axis=-1, keepdims=True) + RMS_EPS) * g


def _pack_complex(re, im):
    hi = lax.bitcast_convert_type(re.astype(BF16).astype(F32), jnp.uint32)
    lo = lax.bitcast_convert_type(im.astype(BF16).astype(F32), jnp.uint32)
    return hi | (lo >> 16)


def _unpack_complex(w):
    re = lax.bitcast_convert_type(w & jnp.uint32(0xFFFF0000), F32)
    im = lax.bitcast_convert_type(w << 16, F32)
    return jnp.concatenate([re, im], axis=0).astype(BF16)


def _granule_transpose(arrs):
    lane = lax.broadcasted_iota(jnp.int32, arrs[0].shape, 1)
    a = list(arrs)
    for d in (4, 2, 1):
        upper = ((lane // S5_H) & d) != 0
        for i in range(S5_GQ):
            if i & d:
                continue
            lo, hi = a[i], a[i + d]
            a[i] = jnp.where(upper, pltpu.roll(hi, d * S5_H, 1), lo)
            a[i + d] = jnp.where(upper, hi, pltpu.roll(lo, FNET_GROUP - d * S5_H, 1))
    return a


def _proj_kernel(x_ref, g_ref, w_ref, fc_ref, vg_ref, ws_ref, sb_ref,
                 xf_ref, sg_ref, c4_ref, c_scr, *, tm):
    h = _rms(x_ref[...], g_ref[...]).astype(BF16)

    a = _dot(h, w_ref[:, 0:BRANCH]).astype(BF16)
    for g in range(FNET_GROUPS):
        lo, hi = g * FNET_GROUP, (g + 1) * FNET_GROUP
        xc = _dot(a[:, lo:hi], fc_ref[...])
        xf_ref[:, lo:hi] = _pack_complex(xc[:, :FNET_GROUP], xc[:, FNET_GROUP:])

    c = _dot(h, w_ref[:, 3 * BRANCH:4 * BRANCH])
    for q in range(S5_Q):
        c_scr[q] = c[:, q * FNET_GROUP:(q + 1) * FNET_GROUP]
    for q in range(S5_Q):
        for half in range(2):
            by_time = [c_scr[q, pl.ds(S5_GQ * half + i, tm // S5_L, stride=S5_L), :]
                       for i in range(S5_GQ)]
            for gl, blk in enumerate(_granule_transpose(by_time)):
                lo = gl * 2 * FNET_GROUP + half * FNET_GROUP
                c4_ref[q, :, lo:lo + FNET_GROUP] = blk

    u = jax.nn.gelu(_dot(h, w_ref[:, BRANCH:2 * BRANCH]))
    v = jax.nn.gelu(_dot(h, w_ref[:, 2 * BRANCH:3 * BRANCH]))
    v = (_rms(v, vg_ref[...])).astype(BF16)
    lane = lax.broadcasted_iota(jnp.int32, (SGU_CHUNK, FNET_GROUP), 1)
    first_head = lane < SGU_HEAD_DIM
    zero = jnp.zeros((SGU_CHUNK, FNET_GROUP), BF16)
    for ck in range(tm // SGU_CHUNK):
        rows = slice(ck * SGU_CHUNK, (ck + 1) * SGU_CHUNK)
        for j in range(SGU_HEADS // 2):
            cols = slice(j * FNET_GROUP, (j + 1) * FNET_GROUP)
            blk = v[rows, cols]
            rhs = jnp.concatenate([jnp.where(first_head, blk, zero),
                                   jnp.where(first_head, zero, blk)], axis=0)
            mixed = _dot(ws_ref[j], rhs) + sb_ref[:, cols]
            sg_ref[rows, cols] = (u[rows, cols] * mixed).astype(BF16)


def _proj(x, g, w, fc, vg, ws, sb):
    t = x.shape[0]
    tm = TOKEN_TILE
    return pl.pallas_call(
        functools.partial(_proj_kernel, tm=tm),
        grid=(t // tm,),
        in_specs=[
            pl.BlockSpec((tm, D_MODEL), lambda i: (i, 0)),
            _const_spec((1, D_MODEL)),
            _const_spec((D_MODEL, 4 * BRANCH)),
            _const_spec((FNET_GROUP, 2 * FNET_GROUP)),
            _const_spec((1, BRANCH)),
            _const_spec((SGU_HEADS // 2, SGU_CHUNK, 2 * SGU_CHUNK)),
            _const_spec((SGU_CHUNK, BRANCH)),
        ],
        out_specs=[
            pl.BlockSpec((tm, BRANCH), lambda i: (i, 0)),
            pl.BlockSpec((tm, BRANCH), lambda i: (i, 0)),
            pl.BlockSpec((S5_Q, tm // S5_L, S5_W), lambda i: (0, i, 0)),
        ],
        out_shape=[
            jax.ShapeDtypeStruct((t, BRANCH), jnp.uint32),
            jax.ShapeDtypeStruct((t, BRANCH), BF16),
            jax.ShapeDtypeStruct((S5_Q, t // S5_L, S5_W), F32),
        ],
        scratch_shapes=[pltpu.VMEM((S5_Q, tm, FNET_GROUP), F32)],
        compiler_params=_cparams("parallel"),
        name="proj",
    )(x, g, w, fc, vg, ws, sb)


def _fft1_kernel(x_ref, f_ref, z_ref, *, n1, sb):
    for j in range(sb):
        z = _dot(f_ref[j], _unpack_complex(x_ref[0, :, j, :]))
        z_ref[0, :, j, :] = _pack_complex(z[:n1], z[n1:])


def _fft1(xf, f1, b, n1):
    sb = 8
    return pl.pallas_call(
        functools.partial(_fft1_kernel, n1=n1, sb=sb),
        grid=(b, FFT_N2 // sb),
        in_specs=[
            pl.BlockSpec((1, n1, sb, BRANCH), lambda i, j: (i, 0, j, 0)),
            pl.BlockSpec((sb, 2 * n1, 2 * n1), lambda i, j: (j, 0, 0)),
        ],
        out_specs=pl.BlockSpec((1, n1, sb, BRANCH), lambda i, j: (i, 0, j, 0)),
        out_shape=jax.ShapeDtypeStruct((b, n1, FFT_N2, BRANCH), jnp.uint32),
        compiler_params=_cparams("parallel", "parallel"),
        name="fft1",
    )(xf.reshape(b, n1, FFT_N2, BRANCH), f1)


def _fft2_kernel(z_ref, f_ref, o_ref, *, kb):
    for k in range(kb):
        o_ref[0, :, k, :] = _dot(f_ref[...], _unpack_complex(z_ref[0, k]))


def _fft2(z, f2, b, n1):
    kb = 8
    return pl.pallas_call(
        functools.partial(_fft2_kernel, kb=kb),
        grid=(b, n1 // kb),
        in_specs=[
            pl.BlockSpec((1, kb, FFT_N2, BRANCH), lambda i, j: (i, j, 0, 0)),
            _const_spec((FFT_N2, 2 * FFT_N2)),
        ],
        out_specs=pl.BlockSpec((1, FFT_N2, kb, BRANCH), lambda i, j: (i, 0, j, 0)),
        out_shape=jax.ShapeDtypeStruct((b, FFT_N2, n1, BRANCH), F32),
        compiler_params=_cparams("parallel", "parallel"),
        name="fft2",
    )(z, f2)


def _s5a_kernel(x_ref, m_ref, e_ref):
    for pp in range(S5_PAIRS):
        e = _dot(x_ref[0, :, pp * S5_PW:(pp + 1) * S5_PW].astype(BF16), m_ref[pp])
        for k in range(4):
            e_ref[0, :, S5_PAIRS * k + pp, :] = e[:, k * FNET_GROUP:(k + 1) * FNET_GROUP]


def _s5a(c4, mend, rows):
    r = min(512, rows)
    return pl.pallas_call(
        _s5a_kernel,
        grid=(S5_Q, rows // r),
        in_specs=[
            pl.BlockSpec((1, r, S5_W), lambda q, i: (q, i, 0)),
            pl.BlockSpec((S5_PAIRS, S5_PW, S5_PW), lambda q, i: (q, 0, 0)),
        ],
        out_specs=pl.BlockSpec((1, r, S5_PIECES, FNET_GROUP), lambda q, i: (q, i, 0, 0)),
        out_shape=jax.ShapeDtypeStruct((S5_Q, rows, S5_PIECES, FNET_GROUP), F32),
        compiler_params=_cparams("parallel", "parallel"),
        name="s5a",
    )(c4, mend)


def _s5b_kernel(a_ref, e_ref, h_ref, *, nc):
    arf, aif, arb, aib = a_ref[0, 0], a_ref[0, 1], a_ref[0, 2], a_ref[0, 3]
    half = S5_PIECES // 2

    def body(n, carry):
        hf, hb = carry
        m = nc - 1 - n
        h_ref[0, n, 0:half, :] = hf
        h_ref[0, m, half:S5_PIECES, :] = hb
        hf = arf * hf + aif * pltpu.roll(hf, half // 2, 0) + e_ref[0, n, 0:half, :]
        hb = arb * hb + aib * pltpu.roll(hb, half // 2, 0) + e_ref[0, m, half:S5_PIECES, :]
        return hf, hb

    zero = jnp.zeros((half, FNET_GROUP), F32)
    lax.fori_loop(0, nc, body, (zero, zero), unroll=4)


def _s5b(e4, decay, b, nc):
    blk = (1, nc, S5_PIECES, FNET_GROUP)
    return pl.pallas_call(
        functools.partial(_s5b_kernel, nc=nc),
        grid=(S5_Q, b),
        in_specs=[
            pl.BlockSpec((1, 4, S5_PIECES // 2, FNET_GROUP), lambda q, i: (q, 0, 0, 0)),
            pl.BlockSpec(blk, lambda q, i: (q * b + i, 0, 0, 0)),
        ],
        out_specs=pl.BlockSpec(blk, lambda q, i: (q * b + i, 0, 0, 0)),
        out_shape=jax.ShapeDtypeStruct((S5_Q * b, nc, S5_PIECES, FNET_GROUP), F32),
        compiler_params=_cparams("parallel", "parallel"),
        name="s5b",
    )(decay, e4.reshape(S5_Q * b, nc, S5_PIECES, FNET_GROUP))


def _s5c_kernel(x_ref, h_ref, t_ref, r_ref, d_ref, y_ref):
    gw = S5_PW // 2
    for pp in range(S5_PAIRS):
        cols = slice(pp * S5_PW, (pp + 1) * S5_PW)
        x = x_ref[0, :, cols]
        xb = x.astype(BF16)
        h = jnp.concatenate([h_ref[0, :, S5_PAIRS * k + pp, :] for k in range(4)], axis=1)
        y = jnp.concatenate([_dot(xb[:, :gw], t_ref[2 * pp]),
                             _dot(xb[:, gw:], t_ref[2 * pp + 1])], axis=1)
        y_ref[0, :, cols] = y + _dot(h.astype(BF16), r_ref[pp]) + x * d_ref[0, :, cols]


def _s5c(c4, h4, toep, rout, dskip, rows):
    r = min(512, rows)
    return pl.pallas_call(
        _s5c_kernel,
        grid=(S5_Q, rows // r),
        in_specs=[
            pl.BlockSpec((1, r, S5_W), lambda q, i: (q, i, 0)),
            pl.BlockSpec((1, r, S5_PIECES, FNET_GROUP), lambda q, i: (q, i, 0, 0)),
            pl.BlockSpec((S5_GQ, S5_PW // 2, S5_PW // 2), lambda q, i: (q, 0, 0)),
            pl.BlockSpec((S5_PAIRS, S5_PW, S5_PW), lambda q, i: (q, 0, 0)),
            pl.BlockSpec((1, 1, S5_W), lambda q, i: (q, 0, 0)),
        ],
        out_specs=pl.BlockSpec((1, r, S5_W), lambda q, i: (q, i, 0)),
        out_shape=jax.ShapeDtypeStruct((S5_Q, rows, S5_W), F32),
        compiler_params=_cparams("parallel", "parallel"),
        name="s5c",
    )(c4, h4, toep, rout, dskip)


def _merge_kernel(x_ref, fo_ref, sg_ref, y4_ref, g_ref, wg_ref, wf_ref, ws_ref,
                  wglu_ref, wo_ref, o_ref, m_ref, y_scr, *, tm):
    x = x_ref[...]
    h = _rms(x, g_ref[...]).astype(BF16)
    fo = fo_ref[...].astype(BF16)
    sg = sg_ref[...]
    for q in range(S5_Q):
        for half in range(2):
            by_group = [y4_ref[q, :, (2 * gl + half) * FNET_GROUP:(2 * gl + half + 1) * FNET_GROUP]
                        for gl in range(S5_GQ)]
            for i, blk in enumerate(_granule_transpose(by_group)):
                y_scr[q, pl.ds(S5_GQ * half + i, tm // S5_L, stride=S5_L), :] = blk
    ys = jax.nn.gelu(jnp.concatenate([y_scr[q] for q in range(S5_Q)], axis=1)).astype(BF16)
    nb = 256
    for c0 in range(0, D_MODEL, nb):
        cs = slice(c0, c0 + nb)
        ga = jax.nn.sigmoid(_dot(h, wg_ref[:, c0:c0 + nb]))
        gb = jax.nn.sigmoid(_dot(h, wg_ref[:, D_MODEL + c0:D_MODEL + c0 + nb]))
        gc = jax.nn.sigmoid(_dot(h, wg_ref[:, 2 * D_MODEL + c0:2 * D_MODEL + c0 + nb]))
        ya = _dot(fo, wf_ref[:, cs])
        yb = _dot(sg, ws_ref[:, cs])
        val = _dot(ys, wglu_ref[:, cs])
        gate = _dot(ys, wglu_ref[:, D_MODEL + c0:D_MODEL + c0 + nb])
        yc = val * jax.nn.sigmoid(gate)
        m_ref[:, cs] = (ga * ya + gb * yb + gc * yc).astype(BF16)
    o_ref[...] = x + _dot(m_ref[...], wo_ref[...])


def _merge(x, fo, sg, y4, g, wg, wf, ws, wglu, wo):
    t = x.shape[0]
    tm = TOKEN_TILE
    return pl.pallas_call(
        functools.partial(_merge_kernel, tm=tm),
        grid=(t // tm,),
        in_specs=[
            pl.BlockSpec((tm, D_MODEL), lambda i: (i, 0)),
            pl.BlockSpec((tm, BRANCH), lambda i: (i, 0)),
            pl.BlockSpec((tm, BRANCH), lambda i: (i, 0)),
            pl.BlockSpec((S5_Q, tm // S5_L, S5_W), lambda i: (0, i, 0)),
            _const_spec((1, D_MODEL)),
            _const_spec((D_MODEL, 3 * D_MODEL)),
            _const_spec((BRANCH, D_MODEL)),
            _const_spec((BRANCH, D_MODEL)),
            _const_spec((BRANCH, 2 * D_MODEL)),
            _const_spec((D_MODEL, D_MODEL)),
        ],
        out_specs=pl.BlockSpec((tm, D_MODEL), lambda i: (i, 0)),
        out_shape=jax.ShapeDtypeStruct((t, D_MODEL), F32),
        scratch_shapes=[pltpu.VMEM((tm, D_MODEL), BF16),
                        pltpu.VMEM((S5_Q, tm, FNET_GROUP), F32)],
        compiler_params=_cparams("parallel"),
        name="merge",
    )(x, fo, sg, y4, g, wg, wf, ws, wglu, wo)


def _ffn_kernel(xp_ref, x_ref, xn_ref, g_ref, wup_ref, cw_ref, wdn_ref, gf_ref,
                o_ref, h_ref, up_ref, act_ref, *, tm, seq, final):
    pos = lax.rem(pl.program_id(0) * tm, seq)
    keep_prev = (pos != 0).astype(F32)
    keep_next = (pos + tm != seq).astype(F32)
    g = g_ref[...]
    h_ref[0:HALO] = (_rms(xp_ref[...], g) * keep_prev).astype(BF16)
    h_ref[HALO:HALO + tm] = _rms(x_ref[...], g).astype(BF16)
    h_ref[HALO + tm:] = (_rms(xn_ref[...], g) * keep_next).astype(BF16)
    hext = h_ref[...]
    for j in range(FF_NBLK):
        up = up_ref.at[j % 2]
        up[...] = _dot(hext, wup_ref[j])
        cw = cw_ref[j]
        conv = (up[pl.ds(HALO - 1, tm), :] * cw[0:1] + up[pl.ds(HALO, tm), :] * cw[1:2]
                + up[pl.ds(HALO + 1, tm), :] * cw[2:3] + cw[3:4])
        act = jax.nn.silu(conv[:, :FF_BLK]) * conv[:, FF_BLK:]
        act_ref[:, j * FF_BLK:(j + 1) * FF_BLK] = act.astype(BF16)
    y = x_ref[...] + _dot(act_ref[...], wdn_ref[...])
    if final:
        y = _rms(y, gf_ref[...])
    o_ref[...] = y


def _ffn(x, g, wup, cw, wdn, gf, seq, final):
    t = x.shape[0]
    tm = FFN_TILE
    per = tm // HALO
    last = t // HALO - 1
    rows = tm + 2 * HALO
    return pl.pallas_call(
        functools.partial(_ffn_kernel, tm=tm, seq=seq, final=final),
        grid=(t // tm,),
        in_specs=[
            pl.BlockSpec((HALO, D_MODEL), lambda i: (jnp.maximum(i * per - 1, 0), 0)),
            pl.BlockSpec((tm, D_MODEL), lambda i: (i, 0)),
            pl.BlockSpec((HALO, D_MODEL), lambda i: (jnp.minimum((i + 1) * per, last), 0)),
            _const_spec((1, D_MODEL)),
            _resident_spec((FF_NBLK, D_MODEL, 2 * FF_BLK)),
            _const_spec((FF_NBLK, 8, 2 * FF_BLK)),
            _resident_spec((D_FF, D_MODEL)),
            _const_spec((1, D_MODEL)),
        ],
        out_specs=pl.BlockSpec((tm, D_MODEL), lambda i: (i, 0)),
        out_shape=jax.ShapeDtypeStruct((t, D_MODEL), F32),
        scratch_shapes=[pltpu.VMEM((rows, D_MODEL), BF16),
                        pltpu.VMEM((2, rows, 2 * FF_BLK), F32),
                        pltpu.VMEM((tm, D_FF), BF16)],
        compiler_params=_cparams("parallel"),
        name="ffn_final" if final else "ffn",
    )(x, x, x, g, wup, cw, wdn, gf)


def _dft_consts(n1):
    n2 = FFT_N2
    s = n1 * n2
    k = np.arange(FNET_GROUP)
    ang = 2 * np.pi * ((k[:, None] * k[None, :]) % FNET_GROUP) / FNET_GROUP
    fc = np.concatenate([np.cos(ang), -np.sin(ang)], axis=1) / math.sqrt(FNET_GROUP)
    k1 = np.arange(n1)[None, :, None]
    s1 = np.arange(n1)[None, None, :]
    s2 = np.arange(n2)[:, None, None]
    ang = 2 * np.pi * ((k1 * (n2 * s1 + s2)) % s) / s
    c, sn = np.cos(ang), np.sin(ang)
    f1 = np.concatenate([np.concatenate([c, sn], axis=2),
                         np.concatenate([-sn, c], axis=2)], axis=1) / math.sqrt(n1)
    k2 = np.arange(n2)
    ang = 2 * np.pi * ((k2[:, None] * k2[None, :]) % n2) / n2
    f2 = np.concatenate([np.cos(ang), np.sin(ang)], axis=1) / math.sqrt(n2)
    return (jnp.asarray(fc, BF16), jnp.asarray(f1, BF16), jnp.asarray(f2, BF16))


def _cmul(ar, ai, br, bi):
    return ar * br - ai * bi, ar * bi + ai * br


def _s5_operators(lam_re, lam_im, log_dt, b_re, b_im, c_re, c_im, d_skip):
    L, G, P, H, Q, GQ = S5_L, S5_G, S5_P, S5_H, S5_Q, S5_GQ
    dt = jnp.exp(log_dt)[..., None]
    lr, li = lam_re * dt, lam_im * dt

    def lb_pow(n):
        n = jnp.asarray(n, F32)[:, None, None, None]
        mag = jnp.exp(n * lr)
        return mag * jnp.cos(n * li), mag * jnp.sin(n * li)

    lbr, lbi = lb_pow([1.0])
    den = lam_re * lam_re + lam_im * lam_im
    bsr, bsi = _cmul(lbr[0] - 1.0, lbi[0], lam_re / den, -lam_im / den)
    cbr, cbi = _cmul(c_re, c_im, bsr[:, :, None, :], bsi[:, :, None, :])

    pw_r, pw_i = lb_pow(np.arange(L + 1))

    def lag_kernel(d):
        wr, wi = _cmul(cbr[d][None], cbi[d][None],
                       pw_r[:L, d][:, :, None, :], pw_i[:L, d][:, :, None, :])
        return (jnp.einsum('sghp,gpk->sghk', wr, b_re)
                - jnp.einsum('sghp,gpk->sghk', wi, b_im))

    kf, kb = lag_kernel(0), lag_kernel(1)
    klag = jnp.concatenate([kb[1:][::-1], (kf[0] + kb[0])[None], kf[1:]], axis=0)
    idx = np.arange(L)[None, :] - np.arange(L)[:, None] + (L - 1)
    kss = klag[idx]
    gw = L * H
    toep = jnp.transpose(kss, (2, 0, 4, 1, 3)).reshape(G, gw, gw)
    pair_eye = jnp.eye(2, dtype=F32)

    def end_state(d, exps):
        er, ei = pw_r[exps, d], pw_i[exps, d]
        return _cmul(er[..., None], ei[..., None], b_re[None], b_im[None])

    fr, fi = end_state(0, np.arange(L)[::-1])
    br_, bi_ = end_state(1, np.arange(L))
    parts = jnp.stack([fr, fi, br_, bi_], axis=0)
    parts = jnp.transpose(parts, (2, 1, 4, 0, 3)).reshape(G // 2, 2, gw, 4, P)
    mend = jnp.einsum('ngrcp,gk->ngrckp', parts, pair_eye).reshape(G // 2, 2 * gw, 4 * 2 * P)

    def read_out(d, exps):
        wr, wi = _cmul(cbr[d][None], cbi[d][None],
                       pw_r[exps, d][:, :, None, :], pw_i[exps, d][:, :, None, :])
        return wr, -wi

    rfr, rfi = read_out(0, np.arange(1, L + 1))
    rbr, rbi = read_out(1, np.arange(L, 0, -1))
    parts = jnp.stack([rfr, rfi, rbr, rbi], axis=0)
    parts = jnp.transpose(parts, (2, 0, 4, 1, 3)).reshape(G // 2, 2, 4, P, gw)
    rout = jnp.einsum('ngcpr,gk->ncgpkr', parts, pair_eye).reshape(G // 2, 4 * 2 * P, 2 * gw)

    ar = pw_r[L].reshape(2, Q, GQ * P // FNET_GROUP, FNET_GROUP)
    ai = pw_i[L].reshape(2, Q, GQ * P // FNET_GROUP, FNET_GROUP)
    a_re = jnp.concatenate([ar, ar], axis=2)
    a_im = jnp.concatenate([-ai, ai], axis=2)
    decay = jnp.stack([a_re[0], a_im[0], a_re[1], a_im[1]], axis=1)

    dtile = jnp.broadcast_to(d_skip[:, None, :], (G, L, H)).reshape(Q, 1, S5_W)
    return mend.astype(BF16), decay, toep.astype(BF16), rout.astype(BF16), dtile


def _pack_sgu(sgu_w, sgu_b):
    ws = jnp.concatenate([sgu_w[0::2], sgu_w[1::2]], axis=2).astype(BF16)
    sb = jnp.repeat(sgu_b.T, SGU_HEAD_DIM, axis=1)
    return ws, sb


def _pack_ffn(w_up, conv_w, conv_b, w_down):
    def blocks(a):
        gate = a[..., :D_FF].reshape(a.shape[:-1] + (FF_NBLK, FF_BLK))
        val = a[..., D_FF:].reshape(a.shape[:-1] + (FF_NBLK, FF_BLK))
        both = jnp.concatenate([gate, val], axis=-1)
        return jnp.moveaxis(both, -2, 0)
    wup = blocks(w_up).astype(BF16)
    taps = jnp.concatenate([conv_w, conv_b[None], jnp.zeros((4, 2 * D_FF), F32)], axis=0)
    cw = blocks(taps)
    wdn = w_down.astype(BF16)
    return wup, cw, wdn


def _trunk(x, layers, g_final, consts):
    b, seq, _ = x.shape
    n1 = seq // FFT_N2
    nc = seq // S5_L
    t = b * seq
    fc, f1, f2 = consts
    x = x.reshape(t, D_MODEL)
    for li, p in enumerate(layers):
        xf, sg, c4 = _proj(x, p["g_mix"], p["w_in_a"], fc, p["v_gain"], p["sgu_w"], p["sgu_b"])
        fo = _fft2(_fft1(xf, f1, b, n1), f2, b, n1).reshape(t, BRANCH)
        e4 = _s5a(c4, p["mend"], b * nc)
        h4 = _s5b(e4, p["decay"], b, nc).reshape(S5_Q, b * nc, S5_PIECES, FNET_GROUP)
        y4 = _s5c(c4, h4, p["toep"], p["rout"], p["dtile"], b * nc)
        x = _merge(x, fo, sg, y4, p["g_mix"], p["w_in_g"], p["w_fnet"], p["w_sgu"],
                   p["w_glu"], p["w_out"])
        x = _ffn(x, p["g_ffn"], p["w_up"], p["conv"], p["w_down"], g_final, seq,
                 final=(li == len(layers) - 1))
    return x.reshape(b, seq, D_MODEL)


def kernel(x_prompt, x_sample, g_mix, w_in, w_fnet, sgu_v_gain, sgu_w, sgu_b, w_sgu,
           s5_lam_re, s5_lam_im, s5_log_dt, s5_b_re, s5_b_im, s5_c_re, s5_c_im, s5_d,
           w_glu, w_out, g_ffn, w_up, conv_w, conv_b, w_down, g_final):
    depth = w_in.shape[0]
    layers = []
    for i in range(depth):
        ws, sb = _pack_sgu(sgu_w[i], sgu_b[i])
        wup, cw, wdn = _pack_ffn(w_up[i], conv_w[i], conv_b[i], w_down[i])
        mend, decay, toep, rout, dtile = _s5_operators(
            s5_lam_re[i], s5_lam_im[i], s5_log_dt[i], s5_b_re[i], s5_b_im[i],
            s5_c_re[i], s5_c_im[i], s5_d[i])
        layers.append(dict(
            g_mix=g_mix[i][None], w_in_a=w_in[i][:, :4 * BRANCH].astype(BF16),
            w_in_g=w_in[i][:, 4 * BRANCH:].astype(BF16),
            v_gain=sgu_v_gain[i][None], sgu_w=ws, sgu_b=sb,
            w_fnet=w_fnet[i].astype(BF16), w_sgu=w_sgu[i].astype(BF16),
            w_glu=w_glu[i].astype(BF16), w_out=w_out[i].astype(BF16),
            mend=mend, decay=decay, toep=toep, rout=rout, dtile=dtile,
            g_ffn=g_ffn[i][None], w_up=wup, conv=cw, w_down=wdn))
    gf = g_final[None]
    outs = []
    for x in (x_prompt, x_sample):
        consts = _dft_consts(x.shape[1] // FFT_N2)
        outs.append(_trunk(x, layers, gf, consts))
    return tuple(outs)
```

```python
import functools
import math

import numpy as np
import jax
import jax.numpy as jnp
from jax import lax
from jax.experimental import pallas as pl
from jax.experimental.pallas import tpu as pltpu

F32 = jnp.float32
BF16 = jnp.bfloat16

D_MODEL = 1024
BRANCH = 512
FNET_GROUP = 128
FNET_GROUPS = BRANCH // FNET_GROUP
SGU_CHUNK = 128
SGU_HEADS = 8
SGU_HEAD_DIM = BRANCH // SGU_HEADS
S5_H = 16
S5_G = BRANCH // S5_H
S5_P = 64
S5_L = 16
S5_Q = 4
S5_GQ = S5_G // S5_Q
S5_W = S5_L * FNET_GROUP
S5_PIECES = S5_W // FNET_GROUP
S5_PAIRS = S5_GQ // 2
S5_PW = 2 * S5_L * S5_H
D_FF = 2816
FF_BLK = 256
FF_NBLK = D_FF // FF_BLK
RMS_EPS = 1e-6
FFT_N2 = 128
HALO = 16

TOKEN_TILE = 512
PROJ_TILE = 1024
FFN_TILE = 1024
VMEM_LIMIT = 60 * 1024 * 1024


def _cparams(*sem):
    return pltpu.CompilerParams(dimension_semantics=sem, vmem_limit_bytes=VMEM_LIMIT)


def _const_spec(shape):
    nd = len(shape)
    return pl.BlockSpec(shape, lambda *_: (0,) * nd)


def _resident_spec(shape):
    nd = len(shape)
    return pl.BlockSpec(shape, lambda *_: (0,) * nd, pipeline_mode=pl.Buffered(1))


def _dot(a, b):
    return jnp.dot(a, b, preferred_element_type=F32)


def _rms(x, g):
    return x * lax.rsqrt(jnp.mean(x * x, axis=-1, keepdims=True) + RMS_EPS) * g


def _pack_complex(re, im):
    hi = lax.bitcast_convert_type(re.astype(BF16).astype(F32), jnp.uint32)
    lo = lax.bitcast_convert_type(im.astype(BF16).astype(F32), jnp.uint32)
    return hi | (lo >> 16)


def _unpack_complex(w):
    re = lax.bitcast_convert_type(w & jnp.uint32(0xFFFF0000), F32)
    im = lax.bitcast_convert_type(w << 16, F32)
    return jnp.concatenate([re, im], axis=0).astype(BF16)


def _granule_transpose(arrs):
    lane = lax.broadcasted_iota(jnp.int32, arrs[0].shape, 1)
    a = list(arrs)
    for d in (4, 2, 1):
        upper = ((lane // S5_H) & d) != 0
        for i in range(S5_GQ):
            if i & d:
                continue
            lo, hi = a[i], a[i + d]
            a[i] = jnp.where(upper, pltpu.roll(hi, d * S5_H, 1), lo)
            a[i + d] = jnp.where(upper, hi, pltpu.roll(lo, FNET_GROUP - d * S5_H, 1))
    return a


def _proj_kernel(x_ref, g_ref, w_ref, fc_ref, vg_ref, ws_ref, sb_ref,
                 h_ref, xf_ref, sg_ref, c4_ref, c_scr, *, tm):
    h = _rms(x_ref[...], g_ref[...]).astype(BF16)
    h_ref[...] = h

    a = _dot(h, w_ref[:, 0:BRANCH]).astype(BF16)
    for g in range(FNET_GROUPS):
        lo, hi = g * FNET_GROUP, (g + 1) * FNET_GROUP
        xc = _dot(a[:, lo:hi], fc_ref[...])
        packed = _pack_complex(xc[:, :FNET_GROUP], xc[:, FNET_GROUP:])
        for j in range(tm // FFT_N2):
            xf_ref[0, :, j, lo:hi] = packed[j * FFT_N2:(j + 1) * FFT_N2]

    c = _dot(h, w_ref[:, 3 * BRANCH:4 * BRANCH])
    for q in range(S5_Q):
        c_scr[q] = c[:, q * FNET_GROUP:(q + 1) * FNET_GROUP]
    for q in range(S5_Q):
        for half in range(2):
            by_time = [c_scr[q, pl.ds(S5_GQ * half + i, tm // S5_L, stride=S5_L), :]
                       for i in range(S5_GQ)]
            for gl, blk in enumerate(_granule_transpose(by_time)):
                lo = gl * 2 * FNET_GROUP + half * FNET_GROUP
                c4_ref[q, :, lo:lo + FNET_GROUP] = blk

    u = jax.nn.gelu(_dot(h, w_ref[:, BRANCH:2 * BRANCH]))
    v = jax.nn.gelu(_dot(h, w_ref[:, 2 * BRANCH:3 * BRANCH]))
    v = (_rms(v, vg_ref[...])).astype(BF16)
    lane = lax.broadcasted_iota(jnp.int32, (SGU_CHUNK, FNET_GROUP), 1)
    first_head = lane < SGU_HEAD_DIM
    zero = jnp.zeros((SGU_CHUNK, FNET_GROUP), BF16)
    for ck in range(tm // SGU_CHUNK):
        rows = slice(ck * SGU_CHUNK, (ck + 1) * SGU_CHUNK)
        for j in range(SGU_HEADS // 2):
            cols = slice(j * FNET_GROUP, (j + 1) * FNET_GROUP)
            blk = v[rows, cols]
            rhs = jnp.concatenate([jnp.where(first_head, blk, zero),
                                   jnp.where(first_head, zero, blk)], axis=0)
            mixed = _dot(ws_ref[j], rhs) + sb_ref[:, cols]
            sg_ref[rows, cols] = (u[rows, cols] * mixed).astype(BF16)


def _proj(x, g, w, fc, vg, ws, sb, b, n1):
    t = x.shape[0]
    tm = PROJ_TILE
    s1_blk = tm // FFT_N2
    per_seq = n1 // s1_blk
    return pl.pallas_call(
        functools.partial(_proj_kernel, tm=tm),
        grid=(t // tm,),
        in_specs=[
            pl.BlockSpec((tm, D_MODEL), lambda i: (i, 0)),
            _const_spec((1, D_MODEL)),
            _resident_spec((D_MODEL, 4 * BRANCH)),
            _const_spec((FNET_GROUP, 2 * FNET_GROUP)),
            _const_spec((1, BRANCH)),
            _const_spec((SGU_HEADS // 2, SGU_CHUNK, 2 * SGU_CHUNK)),
            _const_spec((SGU_CHUNK, BRANCH)),
        ],
        out_specs=[
            pl.BlockSpec((tm, D_MODEL), lambda i: (i, 0)),
            pl.BlockSpec((1, FFT_N2, s1_blk, BRANCH), lambda i: (i // per_seq, 0, i % per_seq, 0)),
            pl.BlockSpec((tm, BRANCH), lambda i: (i, 0)),
            pl.BlockSpec((S5_Q, tm // S5_L, S5_W), lambda i: (0, i, 0)),
        ],
        out_shape=[
            jax.ShapeDtypeStruct((t, D_MODEL), BF16),
            jax.ShapeDtypeStruct((b, FFT_N2, n1, BRANCH), jnp.uint32),
            jax.ShapeDtypeStruct((t, BRANCH), BF16),
            jax.ShapeDtypeStruct((S5_Q, t // S5_L, S5_W), F32),
        ],
        scratch_shapes=[pltpu.VMEM((S5_Q, tm, FNET_GROUP), F32)],
        compiler_params=_cparams("parallel"),
        name="proj",
    )(x, g, w, fc, vg, ws, sb)


def _fft1_kernel(x_ref, f_ref, z_ref, *, n1, sb):
    for j in range(sb):
        z = _dot(f_ref[j], _unpack_complex(x_ref[0, j]))
        z_ref[0, :, j, :] = _pack_complex(z[:n1], z[n1:])


def _fft1(xf, f1, b, n1):
    sb = 8
    return pl.pallas_call(
        functools.partial(_fft1_kernel, n1=n1, sb=sb),
        grid=(b, FFT_N2 // sb),
        in_specs=[
            pl.BlockSpec((1, sb, n1, BRANCH), lambda i, j: (i, j, 0, 0)),
            pl.BlockSpec((sb, 2 * n1, 2 * n1), lambda i, j: (j, 0, 0)),
        ],
        out_specs=pl.BlockSpec((1, n1, sb, BRANCH), lambda i, j: (i, 0, j, 0)),
        out_shape=jax.ShapeDtypeStruct((b, n1, FFT_N2, BRANCH), jnp.uint32),
        compiler_params=_cparams("parallel", "parallel"),
        name="fft1",
    )(xf, f1)


def _fft2_kernel(z_ref, f_ref, o_ref, *, kb):
    for k in range(kb):
        o_ref[0, :, k, :] = _dot(f_ref[...], _unpack_complex(z_ref[0, k]))


def _fft2(z, f2, b, n1):
    kb = 8
    return pl.pallas_call(
        functools.partial(_fft2_kernel, kb=kb),
        grid=(b, n1 // kb),
        in_specs=[
            pl.BlockSpec((1, kb, FFT_N2, BRANCH), lambda i, j: (i, j, 0, 0)),
            _const_spec((FFT_N2, 2 * FFT_N2)),
        ],
        out_specs=pl.BlockSpec((1, FFT_N2, kb, BRANCH), lambda i, j: (i, 0, j, 0)),
        out_shape=jax.ShapeDtypeStruct((b, FFT_N2, n1, BRANCH), F32),
        compiler_params=_cparams("parallel", "parallel"),
        name="fft2",
    )(z, f2)


def _s5a_kernel(x_ref, m_ref, e_ref):
    for pp in range(S5_PAIRS):
        e = _dot(x_ref[0, :, pp * S5_PW:(pp + 1) * S5_PW].astype(BF16), m_ref[pp])
        for k in range(4):
            e_ref[0, :, S5_PAIRS * k + pp, :] = e[:, k * FNET_GROUP:(k + 1) * FNET_GROUP]


def _s5a(c4, mend, rows):
    r = min(512, rows)
    return pl.pallas_call(
        _s5a_kernel,
        grid=(S5_Q, rows // r),
        in_specs=[
            pl.BlockSpec((1, r, S5_W), lambda q, i: (q, i, 0)),
            pl.BlockSpec((S5_PAIRS, S5_PW, S5_PW), lambda q, i: (q, 0, 0)),
        ],
        out_specs=pl.BlockSpec((1, r, S5_PIECES, FNET_GROUP), lambda q, i: (q, i, 0, 0)),
        out_shape=jax.ShapeDtypeStruct((S5_Q, rows, S5_PIECES, FNET_GROUP), F32),
        compiler_params=_cparams("parallel", "parallel"),
        name="s5a",
    )(c4, mend)


def _s5b_kernel(a_ref, e_ref, h_ref, *, nc):
    arf, aif, arb, aib = a_ref[0, 0], a_ref[0, 1], a_ref[0, 2], a_ref[0, 3]
    half = S5_PIECES // 2

    def body(n, carry):
        hf, hb = carry
        m = nc - 1 - n
        h_ref[0, n, 0:half, :] = hf
        h_ref[0, m, half:S5_PIECES, :] = hb
        hf = arf * hf + aif * pltpu.roll(hf, half // 2, 0) + e_ref[0, n, 0:half, :]
        hb = arb * hb + aib * pltpu.roll(hb, half // 2, 0) + e_ref[0, m, half:S5_PIECES, :]
        return hf, hb

    zero = jnp.zeros((half, FNET_GROUP), F32)
    lax.fori_loop(0, nc, body, (zero, zero), unroll=4)


def _s5b(e4, decay, b, nc):
    blk = (1, nc, S5_PIECES, FNET_GROUP)
    return pl.pallas_call(
        functools.partial(_s5b_kernel, nc=nc),
        grid=(S5_Q, b),
        in_specs=[
            pl.BlockSpec((1, 4, S5_PIECES // 2, FNET_GROUP), lambda q, i: (q, 0, 0, 0)),
            pl.BlockSpec(blk, lambda q, i: (q * b + i, 0, 0, 0)),
        ],
        out_specs=pl.BlockSpec(blk, lambda q, i: (q * b + i, 0, 0, 0)),
        out_shape=jax.ShapeDtypeStruct((S5_Q * b, nc, S5_PIECES, FNET_GROUP), F32),
        compiler_params=_cparams("parallel", "parallel"),
        name="s5b",
    )(decay, e4.reshape(S5_Q * b, nc, S5_PIECES, FNET_GROUP))


def _s5c_kernel(x_ref, h_ref, t_ref, r_ref, d_ref, y_ref):
    gw = S5_PW // 2
    for pp in range(S5_PAIRS):
        cols = slice(pp * S5_PW, (pp + 1) * S5_PW)
        x = x_ref[0, :, cols]
        xb = x.astype(BF16)
        h = jnp.concatenate([h_ref[0, :, S5_PAIRS * k + pp, :] for k in range(4)], axis=1)
        y = jnp.concatenate([_dot(xb[:, :gw], t_ref[2 * pp]),
                             _dot(xb[:, gw:], t_ref[2 * pp + 1])], axis=1)
        y_ref[0, :, cols] = y + _dot(h.astype(BF16), r_ref[pp]) + x * d_ref[0, :, cols]


def _s5c(c4, h4, toep, rout, dskip, rows):
    r = min(512, rows)
    return pl.pallas_call(
        _s5c_kernel,
        grid=(S5_Q, rows // r),
        in_specs=[
            pl.BlockSpec((1, r, S5_W), lambda q, i: (q, i, 0)),
            pl.BlockSpec((1, r, S5_PIECES, FNET_GROUP), lambda q, i: (q, i, 0, 0)),
            pl.BlockSpec((S5_GQ, S5_PW // 2, S5_PW // 2), lambda q, i: (q, 0, 0)),
            pl.BlockSpec((S5_PAIRS, S5_PW, S5_PW), lambda q, i: (q, 0, 0)),
            pl.BlockSpec((1, 1, S5_W), lambda q, i: (q, 0, 0)),
        ],
        out_specs=pl.BlockSpec((1, r, S5_W), lambda q, i: (q, i, 0)),
        out_shape=jax.ShapeDtypeStruct((S5_Q, rows, S5_W), F32),
        compiler_params=_cparams("parallel", "parallel"),
        name="s5c",
    )(c4, h4, toep, rout, dskip)


def _merge_kernel(x_ref, h_ref, fo_ref, sg_ref, y4_ref, g_ref, wg_ref, wf_ref, ws_ref,
                  wglu_ref, wo_ref, o_ref, h2_ref, m_ref, y_scr, *, tm):
    h = h_ref[...]
    fo = fo_ref[...].astype(BF16)
    sg = sg_ref[...]
    for q in range(S5_Q):
        for half in range(2):
            by_group = [y4_ref[q, :, (2 * gl + half) * FNET_GROUP:(2 * gl + half + 1) * FNET_GROUP]
                        for gl in range(S5_GQ)]
            for i, blk in enumerate(_granule_transpose(by_group)):
                y_scr[q, pl.ds(S5_GQ * half + i, tm // S5_L, stride=S5_L), :] = blk
    ys = jax.nn.gelu(jnp.concatenate([y_scr[q] for q in range(S5_Q)], axis=1)).astype(BF16)
    nb = 256
    for c0 in range(0, D_MODEL, nb):
        cs = slice(c0, c0 + nb)
        ga = jax.nn.sigmoid(_dot(h, wg_ref[:, c0:c0 + nb]))
        gb = jax.nn.sigmoid(_dot(h, wg_ref[:, D_MODEL + c0:D_MODEL + c0 + nb]))
        gc = jax.nn.sigmoid(_dot(h, wg_ref[:, 2 * D_MODEL + c0:2 * D_MODEL + c0 + nb]))
        ya = _dot(fo, wf_ref[:, cs])
        yb = _dot(sg, ws_ref[:, cs])
        val = _dot(ys, wglu_ref[:, cs])
        gate = _dot(ys, wglu_ref[:, D_MODEL + c0:D_MODEL + c0 + nb])
        yc = val * jax.nn.sigmoid(gate)
        m_ref[:, cs] = (ga * ya + gb * yb + gc * yc).astype(BF16)
    g = g_ref[...]
    for r0 in range(0, tm, SGU_CHUNK):
        rows = slice(r0, r0 + SGU_CHUNK)
        xo = x_ref[rows, :] + _dot(m_ref[rows, :], wo_ref[...])
        o_ref[rows, :] = xo
        h2_ref[rows, :] = _rms(xo, g).astype(BF16)


def _merge(x, h, fo, sg, y4, g_ffn, wg, wf, ws, wglu, wo):
    t = x.shape[0]
    tm = TOKEN_TILE
    return pl.pallas_call(
        functools.partial(_merge_kernel, tm=tm),
        grid=(t // tm,),
        in_specs=[
            pl.BlockSpec((tm, D_MODEL), lambda i: (i, 0)),
            pl.BlockSpec((tm, D_MODEL), lambda i: (i, 0)),
            pl.BlockSpec((tm, BRANCH), lambda i: (i, 0)),
            pl.BlockSpec((tm, BRANCH), lambda i: (i, 0)),
            pl.BlockSpec((S5_Q, tm // S5_L, S5_W), lambda i: (0, i, 0)),
            _const_spec((1, D_MODEL)),
            _const_spec((D_MODEL, 3 * D_MODEL)),
            _const_spec((BRANCH, D_MODEL)),
            _const_spec((BRANCH, D_MODEL)),
            _const_spec((BRANCH, 2 * D_MODEL)),
            _const_spec((D_MODEL, D_MODEL)),
        ],
        out_specs=[pl.BlockSpec((tm, D_MODEL), lambda i: (i, 0)),
                   pl.BlockSpec((tm, D_MODEL), lambda i: (i, 0))],
        out_shape=[jax.ShapeDtypeStruct((t, D_MODEL), F32),
                   jax.ShapeDtypeStruct((t, D_MODEL), BF16)],
        scratch_shapes=[pltpu.VMEM((tm, D_MODEL), BF16),
                        pltpu.VMEM((S5_Q, tm, FNET_GROUP), F32)],
        compiler_params=_cparams("parallel"),
        name="merge",
    )(x, h, fo, sg, y4, g_ffn, wg, wf, ws, wglu, wo)


def _ffn_kernel(hp_ref, hm_ref, hn_ref, x_ref, wup_ref, cw_ref, wdn_ref, gf_ref,
                o_ref, h_ref, up_ref, act_ref, *, tm, seq, final):
    pos = lax.rem(pl.program_id(0) * tm, seq)
    zero = jnp.zeros((HALO, D_MODEL), BF16)
    h_ref[0:HALO] = jnp.where(pos != 0, hp_ref[...], zero)
    h_ref[HALO:HALO + tm] = hm_ref[...]
    h_ref[HALO + tm:] = jnp.where(pos + tm != seq, hn_ref[...], zero)
    hext = h_ref[...]
    for j in range(FF_NBLK):
        up = up_ref.at[j % 2]
        up[...] = _dot(hext, wup_ref[j])
        cw = cw_ref[j]
        conv = (up[pl.ds(HALO - 1, tm), :] * cw[0:1] + up[pl.ds(HALO, tm), :] * cw[1:2]
                + up[pl.ds(HALO + 1, tm), :] * cw[2:3] + cw[3:4])
        act = jax.nn.silu(conv[:, :FF_BLK]) * conv[:, FF_BLK:]
        act_ref[:, j * FF_BLK:(j + 1) * FF_BLK] = act.astype(BF16)
    y = x_ref[...] + _dot(act_ref[...], wdn_ref[...])
    if final:
        y = _rms(y, gf_ref[...])
    o_ref[...] = y


def _ffn(x, h, wup, cw, wdn, gf, seq, final):
    t = x.shape[0]
    tm = FFN_TILE
    per = tm // HALO
    last = t // HALO - 1
    rows = tm + 2 * HALO
    return pl.pallas_call(
        functools.partial(_ffn_kernel, tm=tm, seq=seq, final=final),
        grid=(t // tm,),
        in_specs=[
            pl.BlockSpec((HALO, D_MODEL), lambda i: (jnp.maximum(i * per - 1, 0), 0)),
            pl.BlockSpec((tm, D_MODEL), lambda i: (i, 0)),
            pl.BlockSpec((HALO, D_MODEL), lambda i: (jnp.minimum((i + 1) * per, last), 0)),
            pl.BlockSpec((tm, D_MODEL), lambda i: (i, 0)),
            _resident_spec((FF_NBLK, D_MODEL, 2 * FF_BLK)),
            _const_spec((FF_NBLK, 8, 2 * FF_BLK)),
            _resident_spec((D_FF, D_MODEL)),
            _const_spec((1, D_MODEL)),
        ],
        out_specs=pl.BlockSpec((tm, D_MODEL), lambda i: (i, 0)),
        out_shape=jax.ShapeDtypeStruct((t, D_MODEL), F32),
        scratch_shapes=[pltpu.VMEM((rows, D_MODEL), BF16),
                        pltpu.VMEM((2, rows, 2 * FF_BLK), F32),
                        pltpu.VMEM((tm, D_FF), BF16)],
        compiler_params=_cparams("parallel"),
        name="ffn_final" if final else "ffn",
    )(h, h, h, x, wup, cw, wdn, gf)


def _dft_consts(n1):
    n2 = FFT_N2
    s = n1 * n2
    k = np.arange(FNET_GROUP)
    ang = 2 * np.pi * ((k[:, None] * k[None, :]) % FNET_GROUP) / FNET_GROUP
    fc = np.concatenate([np.cos(ang), -np.sin(ang)], axis=1) / math.sqrt(FNET_GROUP)
    k1 = np.arange(n1)[None, :, None]
    s1 = np.arange(n1)[None, None, :]
    s2 = np.arange(n2)[:, None, None]
    ang = 2 * np.pi * ((k1 * (n2 * s1 + s2)) % s) / s
    c, sn = np.cos(ang), np.sin(ang)
    f1 = np.concatenate([np.concatenate([c, sn], axis=2),
                         np.concatenate([-sn, c], axis=2)], axis=1) / math.sqrt(n1)
    k2 = np.arange(n2)
    ang = 2 * np.pi * ((k2[:, None] * k2[None, :]) % n2) / n2
    f2 = np.concatenate([np.cos(ang), np.sin(ang)], axis=1) / math.sqrt(n2)
    return (jnp.asarray(fc, BF16), jnp.asarray(f1, BF16), jnp.asarray(f2, BF16))


def _cmul(ar, ai, br, bi):
    return ar * br - ai * bi, ar * bi + ai * br


def _s5_operators(lam_re, lam_im, log_dt, b_re, b_im, c_re, c_im, d_skip):
    L, G, P, H, Q, GQ = S5_L, S5_G, S5_P, S5_H, S5_Q, S5_GQ
    dt = jnp.exp(log_dt)[..., None]
    lr, li = lam_re * dt, lam_im * dt

    def lb_pow(n):
        n = jnp.asarray(n, F32)[:, None, None, None]
        mag = jnp.exp(n * lr)
        return mag * jnp.cos(n * li), mag * jnp.sin(n * li)

    lbr, lbi = lb_pow([1.0])
    den = lam_re * lam_re + lam_im * lam_im
    bsr, bsi = _cmul(lbr[0] - 1.0, lbi[0], lam_re / den, -lam_im / den)
    cbr, cbi = _cmul(c_re, c_im, bsr[:, :, None, :], bsi[:, :, None, :])

    pw_r, pw_i = lb_pow(np.arange(L + 1))

    def lag_kernel(d):
        wr, wi = _cmul(cbr[d][None], cbi[d][None],
                       pw_r[:L, d][:, :, None, :], pw_i[:L, d][:, :, None, :])
        return (jnp.einsum('sghp,gpk->sghk', wr, b_re)
                - jnp.einsum('sghp,gpk->sghk', wi, b_im))

    kf, kb = lag_kernel(0), lag_kernel(1)
    klag = jnp.concatenate([kb[1:][::-1], (kf[0] + kb[0])[None], kf[1:]], axis=0)
    idx = np.arange(L)[None, :] - np.arange(L)[:, None] + (L - 1)
    kss = klag[idx]
    gw = L * H
    toep = jnp.transpose(kss, (2, 0, 4, 1, 3)).reshape(G, gw, gw)
    pair_eye = jnp.eye(2, dtype=F32)

    def end_state(d, exps):
        er, ei = pw_r[exps, d], pw_i[exps, d]
        return _cmul(er[..., None], ei[..., None], b_re[None], b_im[None])

    fr, fi = end_state(0, np.arange(L)[::-1])
    br_, bi_ = end_state(1, np.arange(L))
    parts = jnp.stack([fr, fi, br_, bi_], axis=0)
    parts = jnp.transpose(parts, (2, 1, 4, 0, 3)).reshape(G // 2, 2, gw, 4, P)
    mend = jnp.einsum('ngrcp,gk->ngrckp', parts, pair_eye).reshape(G // 2, 2 * gw, 4 * 2 * P)

    def read_out(d, exps):
        wr, wi = _cmul(cbr[d][None], cbi[d][None],
                       pw_r[exps, d][:, :, None, :], pw_i[exps, d][:, :, None, :])
        return wr, -wi

    rfr, rfi = read_out(0, np.arange(1, L + 1))
    rbr, rbi = read_out(1, np.arange(L, 0, -1))
    parts = jnp.stack([rfr, rfi, rbr, rbi], axis=0)
    parts = jnp.transpose(parts, (2, 0, 4, 1, 3)).reshape(G // 2, 2, 4, P, gw)
    rout = jnp.einsum('ngcpr,gk->ncgpkr', parts, pair_eye).reshape(G // 2, 4 * 2 * P, 2 * gw)

    ar = pw_r[L].reshape(2, Q, GQ * P // FNET_GROUP, FNET_GROUP)
    ai = pw_i[L].reshape(2, Q, GQ * P // FNET_GROUP, FNET_GROUP)
    a_re = jnp.concatenate([ar, ar], axis=2)
    a_im = jnp.concatenate([-ai, ai], axis=2)
    decay = jnp.stack([a_re[0], a_im[0], a_re[1], a_im[1]], axis=1)

    dtile = jnp.broadcast_to(d_skip[:, None, :], (G, L, H)).reshape(Q, 1, S5_W)
    return mend.astype(BF16), decay, toep.astype(BF16), rout.astype(BF16), dtile


def _pack_sgu(sgu_w, sgu_b):
    ws = jnp.concatenate([sgu_w[0::2], sgu_w[1::2]], axis=2).astype(BF16)
    sb = jnp.repeat(sgu_b.T, SGU_HEAD_DIM, axis=1)
    return ws, sb


def _pack_ffn(w_up, conv_w, conv_b, w_down):
    def blocks(a):
        gate = a[..., :D_FF].reshape(a.shape[:-1] + (FF_NBLK, FF_BLK))
        val = a[..., D_FF:].reshape(a.shape[:-1] + (FF_NBLK, FF_BLK))
        both = jnp.concatenate([gate, val], axis=-1)
        return jnp.moveaxis(both, -2, 0)
    wup = blocks(w_up).astype(BF16)
    taps = jnp.concatenate([conv_w, conv_b[None], jnp.zeros((4, 2 * D_FF), F32)], axis=0)
    cw = blocks(taps)
    wdn = w_down.astype(BF16)
    return wup, cw, wdn


def _trunk(x, layers, g_final, consts):
    b, seq, _ = x.shape
    n1 = seq // FFT_N2
    nc = seq // S5_L
    t = b * seq
    fc, f1, f2 = consts
    x = x.reshape(t, D_MODEL)
    for li, p in enumerate(layers):
        h, xf, sg, c4 = _proj(x, p["g_mix"], p["w_in_a"], fc, p["v_gain"], p["sgu_w"], p["sgu_b"],
                              b, n1)
        fo = _fft2(_fft1(xf, f1, b, n1), f2, b, n1).reshape(t, BRANCH)
        e4 = _s5a(c4, p["mend"], b * nc)
        h4 = _s5b(e4, p["decay"], b, nc).reshape(S5_Q, b * nc, S5_PIECES, FNET_GROUP)
        y4 = _s5c(c4, h4, p["toep"], p["rout"], p["dtile"], b * nc)
        x, h2 = _merge(x, h, fo, sg, y4, p["g_ffn"], p["w_in_g"], p["w_fnet"], p["w_sgu"],
                       p["w_glu"], p["w_out"])
        x = _ffn(x, h2, p["w_up"], p["conv"], p["w_down"], g_final, seq,
                 final=(li == len(layers) - 1))
    return x.reshape(b, seq, D_MODEL)


def kernel(x_prompt, x_sample, g_mix, w_in, w_fnet, sgu_v_gain, sgu_w, sgu_b, w_sgu,
           s5_lam_re, s5_lam_im, s5_log_dt, s5_b_re, s5_b_im, s5_c_re, s5_c_im, s5_d,
           w_glu, w_out, g_ffn, w_up, conv_w, conv_b, w_down, g_final):
    depth = w_in.shape[0]
    layers = []
    for i in range(depth):
        ws, sb = _pack_sgu(sgu_w[i], sgu_b[i])
        wup, cw, wdn = _pack_ffn(w_up[i], conv_w[i], conv_b[i], w_down[i])
        mend, decay, toep, rout, dtile = _s5_operators(
            s5_lam_re[i], s5_lam_im[i], s5_log_dt[i], s5_b_re[i], s5_b_im[i],
            s5_c_re[i], s5_c_im[i], s5_d[i])
        layers.append(dict(
            g_mix=g_mix[i][None], w_in_a=w_in[i][:, :4 * BRANCH].astype(BF16),
            w_in_g=w_in[i][:, 4 * BRANCH:].astype(BF16),
            v_gain=sgu_v_gain[i][None], sgu_w=ws, sgu_b=sb,
            w_fnet=w_fnet[i].astype(BF16), w_sgu=w_sgu[i].astype(BF16),
            w_glu=w_glu[i].astype(BF16), w_out=w_out[i].astype(BF16),
            mend=mend, decay=decay, toep=toep, rout=rout, dtile=dtile,
            g_ffn=g_ffn[i][None], w_up=wup, conv=cw, w_down=wdn))
    gf = g_final[None]
    outs = []
    for x in (x_prompt, x_sample):
        consts = _dft_consts(x.shape[1] // FFT_N2)
        outs.append(_trunk(x, layers, gf, consts))
    return tuple(outs)
```

```python
import functools
import math

import numpy as np
import jax
import jax.numpy as jnp
from jax import lax
from jax.experimental import pallas as pl
from jax.experimental.pallas import tpu as pltpu

F32 = jnp.float32
BF16 = jnp.bfloat16

D_MODEL = 1024
BRANCH = 512
FNET_GROUP = 128
FNET_GROUPS = BRANCH // FNET_GROUP
SGU_CHUNK = 128
SGU_HEADS = 8
SGU_HEAD_DIM = BRANCH // SGU_HEADS
S5_H = 16
S5_G = BRANCH // S5_H
S5_P = 64
S5_L = 16
S5_Q = 4
S5_GQ = S5_G // S5_Q
S5_W = S5_L * FNET_GROUP
S5_PIECES = S5_W // FNET_GROUP
S5_PAIRS = S5_GQ // 2
S5_PW = 2 * S5_L * S5_H
D_FF = 2816
FF_BLK = 256
FF_NBLK = D_FF // FF_BLK
RMS_EPS = 1e-6
FFT_N2 = 128
HALO = 16

TOKEN_TILE = 512
PROJ_TILE = 1024
FFN_TILE = 1024
VMEM_LIMIT = 60 * 1024 * 1024


def _cparams(*sem):
    return pltpu.CompilerParams(dimension_semantics=sem, vmem_limit_bytes=VMEM_LIMIT)


def _const_spec(shape):
    nd = len(shape)
    return pl.BlockSpec(shape, lambda *_: (0,) * nd)


def _layer_spec(shape, layer, index=None, resident=False):
    idx = (layer,) + ((0,) * len(shape) if index is None else tuple(index))
    kwargs = dict(pipeline_mode=pl.Buffered(1)) if resident else {}
    return pl.BlockSpec((None,) + tuple(shape), lambda *_: idx, **kwargs)


def _dot(a, b):
    return jnp.dot(a, b, preferred_element_type=F32)


def _rms(x, g):
    return x * lax.rsqrt(jnp.mean(x * x, axis=-1, keepdims=True) + RMS_EPS) * g


def _pack_complex(re, im):
    hi = lax.bitcast_convert_type(re.astype(BF16).astype(F32), jnp.uint32)
    lo = lax.bitcast_convert_type(im.astype(BF16).astype(F32), jnp.uint32)
    return hi | (lo >> 16)


def _unpack_complex(w):
    re = lax.bitcast_convert_type(w & jnp.uint32(0xFFFF0000), F32)
    im = lax.bitcast_convert_type(w << 16, F32)
    return jnp.concatenate([re, im], axis=0).astype(BF16)


def _granule_transpose(arrs):
    lane = lax.broadcasted_iota(jnp.int32, arrs[0].shape, 1)
    a = list(arrs)
    for d in (4, 2, 1):
        upper = ((lane // S5_H) & d) != 0
        for i in range(S5_GQ):
            if i & d:
                continue
            lo, hi = a[i], a[i + d]
            a[i] = jnp.where(upper, pltpu.roll(hi, d * S5_H, 1), lo)
            a[i + d] = jnp.where(upper, hi, pltpu.roll(lo, FNET_GROUP - d * S5_H, 1))
    return a


def _proj_kernel(x_ref, g_ref, w_ref, fc_ref, vg_ref, ws_ref, sb_ref,
                 h_ref, xf_ref, sg_ref, c4_ref, c_scr, *, tm):
    h = _rms(x_ref[...], g_ref[...]).astype(BF16)
    h_ref[...] = h

    a = _dot(h, w_ref[:, 0:BRANCH]).astype(BF16)
    for g in range(FNET_GROUPS):
        lo, hi = g * FNET_GROUP, (g + 1) * FNET_GROUP
        xc = _dot(a[:, lo:hi], fc_ref[...])
        packed = _pack_complex(xc[:, :FNET_GROUP], xc[:, FNET_GROUP:])
        for j in range(tm // FFT_N2):
            xf_ref[0, :, j, lo:hi] = packed[j * FFT_N2:(j + 1) * FFT_N2]

    c = _dot(h, w_ref[:, 3 * BRANCH:4 * BRANCH])
    for q in range(S5_Q):
        c_scr[q] = c[:, q * FNET_GROUP:(q + 1) * FNET_GROUP]
    for q in range(S5_Q):
        for half in range(2):
            by_time = [c_scr[q, pl.ds(S5_GQ * half + i, tm // S5_L, stride=S5_L), :]
                       for i in range(S5_GQ)]
            for gl, blk in enumerate(_granule_transpose(by_time)):
                lo = gl * 2 * FNET_GROUP + half * FNET_GROUP
                c4_ref[q, :, lo:lo + FNET_GROUP] = blk.astype(BF16)

    u = jax.nn.gelu(_dot(h, w_ref[:, BRANCH:2 * BRANCH]))
    v = jax.nn.gelu(_dot(h, w_ref[:, 2 * BRANCH:3 * BRANCH]))
    v = (_rms(v, vg_ref[...])).astype(BF16)
    lane = lax.broadcasted_iota(jnp.int32, (SGU_CHUNK, FNET_GROUP), 1)
    first_head = lane < SGU_HEAD_DIM
    zero = jnp.zeros((SGU_CHUNK, FNET_GROUP), BF16)
    for ck in range(tm // SGU_CHUNK):
        rows = slice(ck * SGU_CHUNK, (ck + 1) * SGU_CHUNK)
        for j in range(SGU_HEADS // 2):
            cols = slice(j * FNET_GROUP, (j + 1) * FNET_GROUP)
            blk = v[rows, cols]
            rhs = jnp.concatenate([jnp.where(first_head, blk, zero),
                                   jnp.where(first_head, zero, blk)], axis=0)
            mixed = _dot(ws_ref[j], rhs) + sb_ref[:, cols]
            sg_ref[rows, cols] = (u[rows, cols] * mixed).astype(BF16)


def _proj(x, layer, g, w, fc, vg, ws, sb, b, n1):
    t = x.shape[0]
    tm = PROJ_TILE
    s1_blk = tm // FFT_N2
    per_seq = n1 // s1_blk
    return pl.pallas_call(
        functools.partial(_proj_kernel, tm=tm),
        grid=(t // tm,),
        in_specs=[
            pl.BlockSpec((tm, D_MODEL), lambda i: (i, 0)),
            _layer_spec((1, D_MODEL), layer),
            _layer_spec((D_MODEL, 4 * BRANCH), layer, resident=True),
            _const_spec((FNET_GROUP, 2 * FNET_GROUP)),
            _layer_spec((1, BRANCH), layer),
            _layer_spec((SGU_HEADS // 2, SGU_CHUNK, 2 * SGU_CHUNK), layer),
            _layer_spec((SGU_CHUNK, BRANCH), layer),
        ],
        out_specs=[
            pl.BlockSpec((tm, D_MODEL), lambda i: (i, 0)),
            pl.BlockSpec((1, FFT_N2, s1_blk, BRANCH), lambda i: (i // per_seq, 0, i % per_seq, 0)),
            pl.BlockSpec((tm, BRANCH), lambda i: (i, 0)),
            pl.BlockSpec((S5_Q, tm // S5_L, S5_W), lambda i: (0, i, 0)),
        ],
        out_shape=[
            jax.ShapeDtypeStruct((t, D_MODEL), BF16),
            jax.ShapeDtypeStruct((b, FFT_N2, n1, BRANCH), jnp.uint32),
            jax.ShapeDtypeStruct((t, BRANCH), BF16),
            jax.ShapeDtypeStruct((S5_Q, t // S5_L, S5_W), BF16),
        ],
        scratch_shapes=[pltpu.VMEM((S5_Q, tm, FNET_GROUP), F32)],
        compiler_params=_cparams("parallel"),
        name="proj",
    )(x, g, w, fc, vg, ws, sb)


def _fft1_kernel(x_ref, f_ref, z_ref, *, n1, sb):
    for j in range(sb):
        z = _dot(f_ref[j], _unpack_complex(x_ref[0, j]))
        z_ref[0, :, j, :] = _pack_complex(z[:n1], z[n1:])


def _fft1(xf, f1, b, n1):
    sb = 8
    return pl.pallas_call(
        functools.partial(_fft1_kernel, n1=n1, sb=sb),
        grid=(b, FFT_N2 // sb),
        in_specs=[
            pl.BlockSpec((1, sb, n1, BRANCH), lambda i, j: (i, j, 0, 0)),
            pl.BlockSpec((sb, 2 * n1, 2 * n1), lambda i, j: (j, 0, 0)),
        ],
        out_specs=pl.BlockSpec((1, n1, sb, BRANCH), lambda i, j: (i, 0, j, 0)),
        out_shape=jax.ShapeDtypeStruct((b, n1, FFT_N2, BRANCH), jnp.uint32),
        compiler_params=_cparams("parallel", "parallel"),
        name="fft1",
    )(xf, f1)


def _fft2_kernel(z_ref, f_ref, o_ref, *, kb):
    for k in range(kb):
        o_ref[0, :, k, :] = _dot(f_ref[...], _unpack_complex(z_ref[0, k]))


def _fft2(z, f2, b, n1):
    kb = 8
    return pl.pallas_call(
        functools.partial(_fft2_kernel, kb=kb),
        grid=(b, n1 // kb),
        in_specs=[
            pl.BlockSpec((1, kb, FFT_N2, BRANCH), lambda i, j: (i, j, 0, 0)),
            _const_spec((FFT_N2, 2 * FFT_N2)),
        ],
        out_specs=pl.BlockSpec((1, FFT_N2, kb, BRANCH), lambda i, j: (i, 0, j, 0)),
        out_shape=jax.ShapeDtypeStruct((b, FFT_N2, n1, BRANCH), F32),
        compiler_params=_cparams("parallel", "parallel"),
        name="fft2",
    )(z, f2)


def _s5_kernel(x_ref, a_ref, m_ref, t_ref, r_ref, d_ref, y_ref, st_ref, *, nc):
    half = S5_PIECES // 2
    gw = S5_PW // 2

    for pp in range(S5_PAIRS):
        e = _dot(x_ref[0, :, pp * S5_PW:(pp + 1) * S5_PW], m_ref[pp])
        for k in range(4):
            st_ref[:, S5_PAIRS * k + pp, :] = e[:, k * FNET_GROUP:(k + 1) * FNET_GROUP]

    arf, aif, arb, aib = a_ref[0, 0], a_ref[0, 1], a_ref[0, 2], a_ref[0, 3]

    def body(n, carry):
        hf, hb = carry
        m = nc - 1 - n
        ef = st_ref[n, 0:half, :]
        eb = st_ref[m, half:S5_PIECES, :]
        st_ref[n, 0:half, :] = hf
        st_ref[m, half:S5_PIECES, :] = hb
        hf = arf * hf + aif * pltpu.roll(hf, half // 2, 0) + ef
        hb = arb * hb + aib * pltpu.roll(hb, half // 2, 0) + eb
        return hf, hb

    zero = jnp.zeros((half, FNET_GROUP), F32)
    lax.fori_loop(0, nc, body, (zero, zero), unroll=4)

    for pp in range(S5_PAIRS):
        cols = slice(pp * S5_PW, (pp + 1) * S5_PW)
        x = x_ref[0, :, cols]
        h = jnp.concatenate([st_ref[:, S5_PAIRS * k + pp, :] for k in range(4)], axis=1)
        y = jnp.concatenate([_dot(x[:, :gw], t_ref[2 * pp]),
                             _dot(x[:, gw:], t_ref[2 * pp + 1])], axis=1)
        y = y + _dot(h.astype(BF16), r_ref[pp]) + x.astype(F32) * d_ref[0, :, cols]
        y_ref[0, :, cols] = y.astype(BF16)


def _s5(c4, layer, decay, mend, toep, rout, dskip, b, nc):
    return pl.pallas_call(
        functools.partial(_s5_kernel, nc=nc),
        grid=(S5_Q, b),
        in_specs=[
            pl.BlockSpec((1, nc, S5_W), lambda q, i: (q, i, 0)),
            pl.BlockSpec((None, 1, 4, S5_PIECES // 2, FNET_GROUP), lambda q, i: (layer, q, 0, 0, 0)),
            pl.BlockSpec((None, S5_PAIRS, S5_PW, S5_PW), lambda q, i: (layer, q, 0, 0)),
            pl.BlockSpec((None, S5_GQ, S5_PW // 2, S5_PW // 2), lambda q, i: (layer, q, 0, 0)),
            pl.BlockSpec((None, S5_PAIRS, S5_PW, S5_PW), lambda q, i: (layer, q, 0, 0)),
            pl.BlockSpec((None, 1, 1, S5_W), lambda q, i: (layer, q, 0, 0)),
        ],
        out_specs=pl.BlockSpec((1, nc, S5_W), lambda q, i: (q, i, 0)),
        out_shape=jax.ShapeDtypeStruct((S5_Q, b * nc, S5_W), BF16),
        scratch_shapes=[pltpu.VMEM((nc, S5_PIECES, FNET_GROUP), F32)],
        compiler_params=_cparams("parallel", "parallel"),
        name="s5",
    )(c4, decay, mend, toep, rout, dskip)


def _merge_kernel(x_ref, h_ref, fo_ref, sg_ref, y4_ref, g_ref, wga_ref, wgb_ref, wgc_ref,
                  wf_ref, ws_ref, wglu_ref, wo_ref, o_ref, h2_ref, m_ref, y_scr, *, tm):
    h = h_ref[...]
    fo = fo_ref[...].astype(BF16)
    sg = sg_ref[...]
    for q in range(S5_Q):
        for half in range(2):
            by_group = [y4_ref[q, :, (2 * gl + half) * FNET_GROUP:
                               (2 * gl + half + 1) * FNET_GROUP].astype(F32)
                        for gl in range(S5_GQ)]
            for i, blk in enumerate(_granule_transpose(by_group)):
                y_scr[q, pl.ds(S5_GQ * half + i, tm // S5_L, stride=S5_L), :] = blk
    ys = jax.nn.gelu(jnp.concatenate([y_scr[q] for q in range(S5_Q)], axis=1)).astype(BF16)
    nb = 256
    for c0 in range(0, D_MODEL, nb):
        cs = slice(c0, c0 + nb)
        ga = jax.nn.sigmoid(_dot(h, wga_ref[:, cs]))
        gb = jax.nn.sigmoid(_dot(h, wgb_ref[:, cs]))
        gc = jax.nn.sigmoid(_dot(h, wgc_ref[:, cs]))
        ya = _dot(fo, wf_ref[:, cs])
        yb = _dot(sg, ws_ref[:, cs])
        val = _dot(ys, wglu_ref[:, cs])
        gate = _dot(ys, wglu_ref[:, D_MODEL + c0:D_MODEL + c0 + nb])
        yc = val * jax.nn.sigmoid(gate)
        m_ref[:, cs] = (ga * ya + gb * yb + gc * yc).astype(BF16)
    g = g_ref[...]
    for r0 in range(0, tm, SGU_CHUNK):
        rows = slice(r0, r0 + SGU_CHUNK)
        xo = x_ref[rows, :] + _dot(m_ref[rows, :], wo_ref[...])
        o_ref[rows, :] = xo
        h2_ref[rows, :] = _rms(xo, g).astype(BF16)


def _merge(x, h, fo, sg, y4, layer, g_ffn, w_in, wf, ws, wglu, wo):
    gate0 = 4 * BRANCH // D_MODEL
    t = x.shape[0]
    tm = TOKEN_TILE
    return pl.pallas_call(
        functools.partial(_merge_kernel, tm=tm),
        grid=(t // tm,),
        in_specs=[
            pl.BlockSpec((tm, D_MODEL), lambda i: (i, 0)),
            pl.BlockSpec((tm, D_MODEL), lambda i: (i, 0)),
            pl.BlockSpec((tm, BRANCH), lambda i: (i, 0)),
            pl.BlockSpec((tm, BRANCH), lambda i: (i, 0)),
            pl.BlockSpec((S5_Q, tm // S5_L, S5_W), lambda i: (0, i, 0)),
            _layer_spec((1, D_MODEL), layer),
            _layer_spec((D_MODEL, D_MODEL), layer, index=(0, gate0)),
            _layer_spec((D_MODEL, D_MODEL), layer, index=(0, gate0 + 1)),
            _layer_spec((D_MODEL, D_MODEL), layer, index=(0, gate0 + 2)),
            _layer_spec((BRANCH, D_MODEL), layer),
            _layer_spec((BRANCH, D_MODEL), layer),
            _layer_spec((BRANCH, 2 * D_MODEL), layer),
            _layer_spec((D_MODEL, D_MODEL), layer),
        ],
        out_specs=[pl.BlockSpec((tm, D_MODEL), lambda i: (i, 0)),
                   pl.BlockSpec((tm, D_MODEL), lambda i: (i, 0))],
        out_shape=[jax.ShapeDtypeStruct((t, D_MODEL), F32),
                   jax.ShapeDtypeStruct((t, D_MODEL), BF16)],
        scratch_shapes=[pltpu.VMEM((tm, D_MODEL), BF16),
                        pltpu.VMEM((S5_Q, tm, FNET_GROUP), F32)],
        compiler_params=_cparams("parallel"),
        name="merge",
    )(x, h, fo, sg, y4, g_ffn, w_in, w_in, w_in, wf, ws, wglu, wo)


def _ffn_kernel(hp_ref, hm_ref, hn_ref, x_ref, wup_ref, cw_ref, wdn_ref, gf_ref,
                o_ref, h_ref, up_ref, act_ref, *, tm, seq, final):
    pos = lax.rem(pl.program_id(0) * tm, seq)
    zero = jnp.zeros((HALO, D_MODEL), BF16)
    h_ref[0:HALO] = jnp.where(pos != 0, hp_ref[...], zero)
    h_ref[HALO:HALO + tm] = hm_ref[...]
    h_ref[HALO + tm:] = jnp.where(pos + tm != seq, hn_ref[...], zero)
    hext = h_ref[...]
    for j in range(FF_NBLK):
        gate_cols = slice(j * FF_BLK, (j + 1) * FF_BLK)
        val_cols = slice(D_FF + j * FF_BLK, D_FF + (j + 1) * FF_BLK)
        up = up_ref.at[j % 2]
        up[:, :FF_BLK] = _dot(hext, wup_ref[:, gate_cols])
        up[:, FF_BLK:] = _dot(hext, wup_ref[:, val_cols])
        cw = jnp.concatenate([cw_ref[:, gate_cols], cw_ref[:, val_cols]], axis=1)
        conv = (up[pl.ds(HALO - 1, tm), :] * cw[0:1] + up[pl.ds(HALO, tm), :] * cw[1:2]
                + up[pl.ds(HALO + 1, tm), :] * cw[2:3] + cw[3:4])
        act = jax.nn.silu(conv[:, :FF_BLK]) * conv[:, FF_BLK:]
        act_ref[:, j * FF_BLK:(j + 1) * FF_BLK] = act.astype(BF16)
    y = x_ref[...] + _dot(act_ref[...], wdn_ref[...])
    if final:
        y = _rms(y, gf_ref[...])
    o_ref[...] = y


def _ffn(x, h, layer, wup, cw, wdn, gf, seq, final):
    t = x.shape[0]
    tm = FFN_TILE
    per = tm // HALO
    last = t // HALO - 1
    rows = tm + 2 * HALO
    return pl.pallas_call(
        functools.partial(_ffn_kernel, tm=tm, seq=seq, final=final),
        grid=(t // tm,),
        in_specs=[
            pl.BlockSpec((HALO, D_MODEL), lambda i: (jnp.maximum(i * per - 1, 0), 0)),
            pl.BlockSpec((tm, D_MODEL), lambda i: (i, 0)),
            pl.BlockSpec((HALO, D_MODEL), lambda i: (jnp.minimum((i + 1) * per, last), 0)),
            pl.BlockSpec((tm, D_MODEL), lambda i: (i, 0)),
            _layer_spec((D_MODEL, 2 * D_FF), layer, resident=True),
            _layer_spec((8, 2 * D_FF), layer),
            _layer_spec((D_FF, D_MODEL), layer, resident=True),
            _const_spec((1, D_MODEL)),
        ],
        out_specs=pl.BlockSpec((tm, D_MODEL), lambda i: (i, 0)),
        out_shape=jax.ShapeDtypeStruct((t, D_MODEL), F32),
        scratch_shapes=[pltpu.VMEM((rows, D_MODEL), BF16),
                        pltpu.VMEM((2, rows, 2 * FF_BLK), F32),
                        pltpu.VMEM((tm, D_FF), BF16)],
        compiler_params=_cparams("parallel"),
        name="ffn_final" if final else "ffn",
    )(h, h, h, x, wup, cw, wdn, gf)


def _dft_consts(n1):
    n2 = FFT_N2
    s = n1 * n2
    k = np.arange(FNET_GROUP)
    ang = 2 * np.pi * ((k[:, None] * k[None, :]) % FNET_GROUP) / FNET_GROUP
    fc = np.concatenate([np.cos(ang), -np.sin(ang)], axis=1) / math.sqrt(FNET_GROUP)
    k1 = np.arange(n1)[None, :, None]
    s1 = np.arange(n1)[None, None, :]
    s2 = np.arange(n2)[:, None, None]
    ang = 2 * np.pi * ((k1 * (n2 * s1 + s2)) % s) / s
    c, sn = np.cos(ang), np.sin(ang)
    f1 = np.concatenate([np.concatenate([c, sn], axis=2),
                         np.concatenate([-sn, c], axis=2)], axis=1) / math.sqrt(n1)
    k2 = np.arange(n2)
    ang = 2 * np.pi * ((k2[:, None] * k2[None, :]) % n2) / n2
    f2 = np.concatenate([np.cos(ang), np.sin(ang)], axis=1) / math.sqrt(n2)
    return (jnp.asarray(fc, BF16), jnp.asarray(f1, BF16), jnp.asarray(f2, BF16))


def _cmul(ar, ai, br, bi):
    return ar * br - ai * bi, ar * bi + ai * br


def _s5_operators(lam_re, lam_im, log_dt, b_re, b_im, c_re, c_im, d_skip):
    L, G, P, H, Q, GQ = S5_L, S5_G, S5_P, S5_H, S5_Q, S5_GQ
    dt = jnp.exp(log_dt)[..., None]
    lr, li = lam_re * dt, lam_im * dt

    def lb_pow(n):
        n = jnp.asarray(n, F32)[:, None, None, None]
        mag = jnp.exp(n * lr)
        return mag * jnp.cos(n * li), mag * jnp.sin(n * li)

    lbr, lbi = lb_pow([1.0])
    den = lam_re * lam_re + lam_im * lam_im
    bsr, bsi = _cmul(lbr[0] - 1.0, lbi[0], lam_re / den, -lam_im / den)
    cbr, cbi = _cmul(c_re, c_im, bsr[:, :, None, :], bsi[:, :, None, :])

    pw_r, pw_i = lb_pow(np.arange(L + 1))

    def lag_kernel(d):
        wr, wi = _cmul(cbr[d][None], cbi[d][None],
                       pw_r[:L, d][:, :, None, :], pw_i[:L, d][:, :, None, :])
        return (jnp.einsum('sghp,gpk->sghk', wr, b_re)
                - jnp.einsum('sghp,gpk->sghk', wi, b_im))

    kf, kb = lag_kernel(0), lag_kernel(1)
    klag = jnp.concatenate([kb[1:][::-1], (kf[0] + kb[0])[None], kf[1:]], axis=0)
    idx = np.arange(L)[None, :] - np.arange(L)[:, None] + (L - 1)
    kss = klag[idx]
    gw = L * H
    toep = jnp.transpose(kss, (2, 0, 4, 1, 3)).reshape(G, gw, gw)
    pair_eye = jnp.eye(2, dtype=F32)

    def end_state(d, exps):
        er, ei = pw_r[exps, d], pw_i[exps, d]
        return _cmul(er[..., None], ei[..., None], b_re[None], b_im[None])

    fr, fi = end_state(0, np.arange(L)[::-1])
    br_, bi_ = end_state(1, np.arange(L))
    parts = jnp.stack([fr, fi, br_, bi_], axis=0)
    parts = jnp.transpose(parts, (2, 1, 4, 0, 3)).reshape(G // 2, 2, gw, 4, P)
    mend = jnp.einsum('ngrcp,gk->ngrckp', parts, pair_eye).reshape(G // 2, 2 * gw, 4 * 2 * P)

    def read_out(d, exps):
        wr, wi = _cmul(cbr[d][None], cbi[d][None],
                       pw_r[exps, d][:, :, None, :], pw_i[exps, d][:, :, None, :])
        return wr, -wi

    rfr, rfi = read_out(0, np.arange(1, L + 1))
    rbr, rbi = read_out(1, np.arange(L, 0, -1))
    parts = jnp.stack([rfr, rfi, rbr, rbi], axis=0)
    parts = jnp.transpose(parts, (2, 0, 4, 1, 3)).reshape(G // 2, 2, 4, P, gw)
    rout = jnp.einsum('ngcpr,gk->ncgpkr', parts, pair_eye).reshape(G // 2, 4 * 2 * P, 2 * gw)

    ar = pw_r[L].reshape(2, Q, GQ * P // FNET_GROUP, FNET_GROUP)
    ai = pw_i[L].reshape(2, Q, GQ * P // FNET_GROUP, FNET_GROUP)
    a_re = jnp.concatenate([ar, ar], axis=2)
    a_im = jnp.concatenate([-ai, ai], axis=2)
    decay = jnp.stack([a_re[0], a_im[0], a_re[1], a_im[1]], axis=1)

    dtile = jnp.broadcast_to(d_skip[:, None, :], (G, L, H)).reshape(Q, 1, S5_W)
    return mend.astype(BF16), decay, toep.astype(BF16), rout.astype(BF16), dtile


def _pack_sgu(sgu_w, sgu_b):
    ws = jnp.concatenate([sgu_w[0::2], sgu_w[1::2]], axis=2).astype(BF16)
    sb = jnp.repeat(sgu_b.T, SGU_HEAD_DIM, axis=1)
    return ws, sb


def _trunk(x, p, consts):
    b, seq, _ = x.shape
    n1 = seq // FFT_N2
    nc = seq // S5_L
    t = b * seq
    fc, f1, f2 = consts
    depth = p["w_in"].shape[0]
    x = x.reshape(t, D_MODEL)
    for li in range(depth):
        h, xf, sg, c4 = _proj(x, li, p["g_mix"], p["w_in"], fc, p["v_gain"], p["sgu_w"],
                              p["sgu_b"], b, n1)
        fo = _fft2(_fft1(xf, f1, b, n1), f2, b, n1).reshape(t, BRANCH)
        y4 = _s5(c4, li, p["decay"], p["mend"], p["toep"], p["rout"], p["dtile"], b, nc)
        x, h2 = _merge(x, h, fo, sg, y4, li, p["g_ffn"], p["w_in"], p["w_fnet"], p["w_sgu"],
                       p["w_glu"], p["w_out"])
        x = _ffn(x, h2, li, p["w_up"], p["conv"], p["w_down"], p["g_final"], seq,
                 final=(li == depth - 1))
    return x.reshape(b, seq, D_MODEL)


def kernel(x_prompt, x_sample, g_mix, w_in, w_fnet, sgu_v_gain, sgu_w, sgu_b, w_sgu,
           s5_lam_re, s5_lam_im, s5_log_dt, s5_b_re, s5_b_im, s5_c_re, s5_c_im, s5_d,
           w_glu, w_out, g_ffn, w_up, conv_w, conv_b, w_down, g_final):
    depth = w_in.shape[0]
    ws, sb = jax.vmap(_pack_sgu)(sgu_w, sgu_b)
    mend, decay, toep, rout, dtile = jax.vmap(_s5_operators)(
        s5_lam_re, s5_lam_im, s5_log_dt, s5_b_re, s5_b_im, s5_c_re, s5_c_im, s5_d)
    conv = jnp.concatenate([conv_w, conv_b[:, None], jnp.zeros((depth, 4, 2 * D_FF), F32)], axis=1)
    params = dict(
        g_mix=g_mix[:, None], w_in=w_in.astype(BF16), v_gain=sgu_v_gain[:, None],
        sgu_w=ws, sgu_b=sb, w_fnet=w_fnet.astype(BF16), w_sgu=w_sgu.astype(BF16),
        w_glu=w_glu.astype(BF16), w_out=w_out.astype(BF16),
        mend=mend, decay=decay, toep=toep, rout=rout, dtile=dtile,
        g_ffn=g_ffn[:, None], w_up=w_up.astype(BF16), conv=conv, w_down=w_down.astype(BF16),
        g_final=g_final[None])
    outs = []
    for x in (x_prompt, x_sample):
        outs.append(_trunk(x, params, _dft_consts(x.shape[1] // FFT_N2)))
    return tuple(outs)
```

```python
import functools
import math

import numpy as np
import jax
import jax.numpy as jnp
from jax import lax
from jax.experimental import pallas as pl
from jax.experimental.pallas import tpu as pltpu

F32 = jnp.float32
BF16 = jnp.bfloat16

D_MODEL = 1024
BRANCH = 512
FNET_GROUP = 128
FNET_GROUPS = BRANCH // FNET_GROUP
SGU_CHUNK = 128
SGU_HEADS = 8
SGU_HEAD_DIM = BRANCH // SGU_HEADS
S5_H = 16
S5_G = BRANCH // S5_H
S5_P = 64
S5_L = 16
S5_Q = 4
S5_GQ = S5_G // S5_Q
S5_W = S5_L * FNET_GROUP
S5_PIECES = S5_W // FNET_GROUP
S5_PAIRS = S5_GQ // 2
S5_PW = 2 * S5_L * S5_H
D_FF = 2816
FF_BLK = 256
FF_NBLK = D_FF // FF_BLK
RMS_EPS = 1e-6
FFT_N2 = 128
HALO = 16

TOKEN_TILE = 512
PROJ_TILE = 1024
FFN_TILE = 1024
VMEM_LIMIT = 60 * 1024 * 1024


def _cparams(*sem):
    return pltpu.CompilerParams(dimension_semantics=sem, vmem_limit_bytes=VMEM_LIMIT)


def _const_spec(shape):
    nd = len(shape)
    return pl.BlockSpec(shape, lambda *_: (0,) * nd)


def _layer_spec(shape, layer, index=None, resident=False):
    idx = (layer,) + ((0,) * len(shape) if index is None else tuple(index))
    kwargs = dict(pipeline_mode=pl.Buffered(1)) if resident else {}
    return pl.BlockSpec((None,) + tuple(shape), lambda *_: idx, **kwargs)


def _dot(a, b):
    return jnp.dot(a, b, preferred_element_type=F32)


def _rms(x, g):
    return x * lax.rsqrt(jnp.mean(x * x, axis=-1, keepdims=True) + RMS_EPS) * g


def _pack_complex(re, im):
    return pltpu.pack_elementwise([re, im], packed_dtype=BF16)


def _unpack_complex(w):
    re = pltpu.unpack_elementwise(w, index=0, packed_dtype=BF16, unpacked_dtype=F32)
    im = pltpu.unpack_elementwise(w, index=1, packed_dtype=BF16, unpacked_dtype=F32)
    return jnp.concatenate([re, im], axis=0).astype(BF16)


def _granule_transpose(arrs):
    lane = lax.broadcasted_iota(jnp.int32, arrs[0].shape, 1)
    a = list(arrs)
    for d in (4, 2, 1):
        upper = ((lane // S5_H) & d) != 0
        for i in range(S5_GQ):
            if i & d:
                continue
            lo, hi = a[i], a[i + d]
            a[i] = jnp.where(upper, pltpu.roll(hi, d * S5_H, 1), lo)
            a[i + d] = jnp.where(upper, hi, pltpu.roll(lo, FNET_GROUP - d * S5_H, 1))
    return a


def _proj_kernel(x_ref, g_ref, w_ref, fc_ref, vg_ref, ws_ref, sb_ref,
                 h_ref, xf_ref, sg_ref, c4_ref, c_scr, *, tm):
    h = _rms(x_ref[...], g_ref[...]).astype(BF16)
    h_ref[...] = h

    a = _dot(h, w_ref[:, 0:BRANCH]).astype(BF16)
    for g in range(FNET_GROUPS):
        lo, hi = g * FNET_GROUP, (g + 1) * FNET_GROUP
        xc = _dot(a[:, lo:hi], fc_ref[...])
        packed = _pack_complex(xc[:, :FNET_GROUP], xc[:, FNET_GROUP:])
        for j in range(tm // FFT_N2):
            xf_ref[0, :, j, lo:hi] = packed[j * FFT_N2:(j + 1) * FFT_N2]

    c = _dot(h, w_ref[:, 3 * BRANCH:4 * BRANCH])
    for q in range(S5_Q):
        c_scr[q] = c[:, q * FNET_GROUP:(q + 1) * FNET_GROUP]
    for q in range(S5_Q):
        for half in range(2):
            by_time = [c_scr[q, pl.ds(S5_GQ * half + i, tm // S5_L, stride=S5_L), :]
                       for i in range(S5_GQ)]
            for gl, blk in enumerate(_granule_transpose(by_time)):
                lo = gl * 2 * FNET_GROUP + half * FNET_GROUP
                c4_ref[q, :, lo:lo + FNET_GROUP] = blk.astype(BF16)

    u = jax.nn.gelu(_dot(h, w_ref[:, BRANCH:2 * BRANCH]))
    v = jax.nn.gelu(_dot(h, w_ref[:, 2 * BRANCH:3 * BRANCH]))
    v = (_rms(v, vg_ref[...])).astype(BF16)
    lane = lax.broadcasted_iota(jnp.int32, (SGU_CHUNK, FNET_GROUP), 1)
    first_head = lane < SGU_HEAD_DIM
    zero = jnp.zeros((SGU_CHUNK, FNET_GROUP), BF16)
    for ck in range(tm // SGU_CHUNK):
        rows = slice(ck * SGU_CHUNK, (ck + 1) * SGU_CHUNK)
        for j in range(SGU_HEADS // 2):
            cols = slice(j * FNET_GROUP, (j + 1) * FNET_GROUP)
            blk = v[rows, cols]
            rhs = jnp.concatenate([jnp.where(first_head, blk, zero),
                                   jnp.where(first_head, zero, blk)], axis=0)
            mixed = _dot(ws_ref[j], rhs) + sb_ref[:, cols]
            sg_ref[rows, cols] = (u[rows, cols] * mixed).astype(BF16)


def _proj(x, layer, g, w, fc, vg, ws, sb, b, n1):
    t = x.shape[0]
    tm = PROJ_TILE
    s1_blk = tm // FFT_N2
    per_seq = n1 // s1_blk
    return pl.pallas_call(
        functools.partial(_proj_kernel, tm=tm),
        grid=(t // tm,),
        in_specs=[
            pl.BlockSpec((tm, D_MODEL), lambda i: (i, 0)),
            _layer_spec((1, D_MODEL), layer),
            _layer_spec((D_MODEL, 4 * BRANCH), layer, resident=True),
            _const_spec((FNET_GROUP, 2 * FNET_GROUP)),
            _layer_spec((1, BRANCH), layer),
            _layer_spec((SGU_HEADS // 2, SGU_CHUNK, 2 * SGU_CHUNK), layer),
            _layer_spec((SGU_CHUNK, BRANCH), layer),
        ],
        out_specs=[
            pl.BlockSpec((tm, D_MODEL), lambda i: (i, 0)),
            pl.BlockSpec((1, FFT_N2, s1_blk, BRANCH), lambda i: (i // per_seq, 0, i % per_seq, 0)),
            pl.BlockSpec((tm, BRANCH), lambda i: (i, 0)),
            pl.BlockSpec((S5_Q, tm // S5_L, S5_W), lambda i: (0, i, 0)),
        ],
        out_shape=[
            jax.ShapeDtypeStruct((t, D_MODEL), BF16),
            jax.ShapeDtypeStruct((b, FFT_N2, n1, BRANCH), jnp.uint32),
            jax.ShapeDtypeStruct((t, BRANCH), BF16),
            jax.ShapeDtypeStruct((S5_Q, t // S5_L, S5_W), BF16),
        ],
        scratch_shapes=[pltpu.VMEM((S5_Q, tm, FNET_GROUP), F32)],
        compiler_params=_cparams("parallel"),
        name="proj",
    )(x, g, w, fc, vg, ws, sb)


def _fft1_kernel(x_ref, f_ref, z_ref, *, n1, sb):
    for j in range(sb):
        z = _dot(f_ref[j], _unpack_complex(x_ref[0, j]))
        z_ref[0, :, j, :] = _pack_complex(z[:n1], z[n1:])


def _fft1(xf, f1, b, n1):
    sb = 8
    return pl.pallas_call(
        functools.partial(_fft1_kernel, n1=n1, sb=sb),
        grid=(b, FFT_N2 // sb),
        in_specs=[
            pl.BlockSpec((1, sb, n1, BRANCH), lambda i, j: (i, j, 0, 0)),
            pl.BlockSpec((sb, 2 * n1, 2 * n1), lambda i, j: (j, 0, 0)),
        ],
        out_specs=pl.BlockSpec((1, n1, sb, BRANCH), lambda i, j: (i, 0, j, 0)),
        out_shape=jax.ShapeDtypeStruct((b, n1, FFT_N2, BRANCH), jnp.uint32),
        compiler_params=_cparams("parallel", "parallel"),
        name="fft1",
    )(xf, f1)


def _fft2_kernel(z_ref, f_ref, o_ref, *, kb):
    for k in range(kb):
        o_ref[0, :, k, :] = _dot(f_ref[...], _unpack_complex(z_ref[0, k]))


def _fft2(z, f2, b, n1):
    kb = 8
    return pl.pallas_call(
        functools.partial(_fft2_kernel, kb=kb),
        grid=(b, n1 // kb),
        in_specs=[
            pl.BlockSpec((1, kb, FFT_N2, BRANCH), lambda i, j: (i, j, 0, 0)),
            _const_spec((FFT_N2, 2 * FFT_N2)),
        ],
        out_specs=pl.BlockSpec((1, FFT_N2, kb, BRANCH), lambda i, j: (i, 0, j, 0)),
        out_shape=jax.ShapeDtypeStruct((b, FFT_N2, n1, BRANCH), F32),
        compiler_params=_cparams("parallel", "parallel"),
        name="fft2",
    )(z, f2)


def _s5_kernel(x_ref, a_ref, m_ref, t_ref, r_ref, d_ref, y_ref, st_ref, *, nc):
    half = S5_PIECES // 2
    gw = S5_PW // 2

    for pp in range(S5_PAIRS):
        e = _dot(x_ref[0, :, pp * S5_PW:(pp + 1) * S5_PW], m_ref[pp])
        for k in range(4):
            st_ref[:, S5_PAIRS * k + pp, :] = e[:, k * FNET_GROUP:(k + 1) * FNET_GROUP]

    arf, aif, arb, aib = a_ref[0, 0], a_ref[0, 1], a_ref[0, 2], a_ref[0, 3]

    def body(n, carry):
        hf, hb = carry
        m = nc - 1 - n
        ef = st_ref[n, 0:half, :]
        eb = st_ref[m, half:S5_PIECES, :]
        st_ref[n, 0:half, :] = hf
        st_ref[m, half:S5_PIECES, :] = hb
        hf = arf * hf + aif * pltpu.roll(hf, half // 2, 0) + ef
        hb = arb * hb + aib * pltpu.roll(hb, half // 2, 0) + eb
        return hf, hb

    zero = jnp.zeros((half, FNET_GROUP), F32)
    lax.fori_loop(0, nc, body, (zero, zero), unroll=4)

    for pp in range(S5_PAIRS):
        cols = slice(pp * S5_PW, (pp + 1) * S5_PW)
        x = x_ref[0, :, cols]
        h = jnp.concatenate([st_ref[:, S5_PAIRS * k + pp, :] for k in range(4)], axis=1)
        y = jnp.concatenate([_dot(x[:, :gw], t_ref[2 * pp]),
                             _dot(x[:, gw:], t_ref[2 * pp + 1])], axis=1)
        y = y + _dot(h.astype(BF16), r_ref[pp]) + x.astype(F32) * d_ref[0, :, cols]
        y_ref[0, :, cols] = y.astype(BF16)


def _s5(c4, layer, decay, mend, toep, rout, dskip, b, nc):
    return pl.pallas_call(
        functools.partial(_s5_kernel, nc=nc),
        grid=(S5_Q, b),
        in_specs=[
            pl.BlockSpec((1, nc, S5_W), lambda q, i: (q, i, 0)),
            pl.BlockSpec((None, 1, 4, S5_PIECES // 2, FNET_GROUP), lambda q, i: (layer, q, 0, 0, 0)),
            pl.BlockSpec((None, S5_PAIRS, S5_PW, S5_PW), lambda q, i: (layer, q, 0, 0)),
            pl.BlockSpec((None, S5_GQ, S5_PW // 2, S5_PW // 2), lambda q, i: (layer, q, 0, 0)),
            pl.BlockSpec((None, S5_PAIRS, S5_PW, S5_PW), lambda q, i: (layer, q, 0, 0)),
            pl.BlockSpec((None, 1, 1, S5_W), lambda q, i: (layer, q, 0, 0)),
        ],
        out_specs=pl.BlockSpec((1, nc, S5_W), lambda q, i: (q, i, 0)),
        out_shape=jax.ShapeDtypeStruct((S5_Q, b * nc, S5_W), BF16),
        scratch_shapes=[pltpu.VMEM((nc, S5_PIECES, FNET_GROUP), F32)],
        compiler_params=_cparams("parallel", "parallel"),
        name="s5",
    )(c4, decay, mend, toep, rout, dskip)


def _merge_kernel(x_ref, h_ref, fo_ref, sg_ref, y4_ref, g_ref, wga_ref, wgb_ref, wgc_ref,
                  wf_ref, ws_ref, wglu_ref, wo_ref, o_ref, h2_ref, m_ref, y_scr, *, tm):
    h = h_ref[...]
    fo = fo_ref[...].astype(BF16)
    sg = sg_ref[...]
    for q in range(S5_Q):
        for half in range(2):
            by_group = [y4_ref[q, :, (2 * gl + half) * FNET_GROUP:
                               (2 * gl + half + 1) * FNET_GROUP].astype(F32)
                        for gl in range(S5_GQ)]
            for i, blk in enumerate(_granule_transpose(by_group)):
                y_scr[q, pl.ds(S5_GQ * half + i, tm // S5_L, stride=S5_L), :] = blk
    ys = jax.nn.gelu(jnp.concatenate([y_scr[q] for q in range(S5_Q)], axis=1)).astype(BF16)
    nb = 256
    for c0 in range(0, D_MODEL, nb):
        cs = slice(c0, c0 + nb)
        ga = jax.nn.sigmoid(_dot(h, wga_ref[:, cs]))
        gb = jax.nn.sigmoid(_dot(h, wgb_ref[:, cs]))
        gc = jax.nn.sigmoid(_dot(h, wgc_ref[:, cs]))
        ya = _dot(fo, wf_ref[:, cs])
        yb = _dot(sg, ws_ref[:, cs])
        val = _dot(ys, wglu_ref[:, cs])
        gate = _dot(ys, wglu_ref[:, D_MODEL + c0:D_MODEL + c0 + nb])
        yc = val * jax.nn.sigmoid(gate)
        m_ref[:, cs] = (ga * ya + gb * yb + gc * yc).astype(BF16)
    g = g_ref[...]
    for r0 in range(0, tm, SGU_CHUNK):
        rows = slice(r0, r0 + SGU_CHUNK)
        xo = x_ref[rows, :] + _dot(m_ref[rows, :], wo_ref[...])
        o_ref[rows, :] = xo
        h2_ref[rows, :] = _rms(xo, g).astype(BF16)


def _merge(x, h, fo, sg, y4, layer, g_ffn, w_in, wf, ws, wglu, wo):
    gate0 = 4 * BRANCH // D_MODEL
    t = x.shape[0]
    tm = TOKEN_TILE
    return pl.pallas_call(
        functools.partial(_merge_kernel, tm=tm),
        grid=(t // tm,),
        in_specs=[
            pl.BlockSpec((tm, D_MODEL), lambda i: (i, 0)),
            pl.BlockSpec((tm, D_MODEL), lambda i: (i, 0)),
            pl.BlockSpec((tm, BRANCH), lambda i: (i, 0)),
            pl.BlockSpec((tm, BRANCH), lambda i: (i, 0)),
            pl.BlockSpec((S5_Q, tm // S5_L, S5_W), lambda i: (0, i, 0)),
            _layer_spec((1, D_MODEL), layer),
            _layer_spec((D_MODEL, D_MODEL), layer, index=(0, gate0)),
            _layer_spec((D_MODEL, D_MODEL), layer, index=(0, gate0 + 1)),
            _layer_spec((D_MODEL, D_MODEL), layer, index=(0, gate0 + 2)),
            _layer_spec((BRANCH, D_MODEL), layer),
            _layer_spec((BRANCH, D_MODEL), layer),
            _layer_spec((BRANCH, 2 * D_MODEL), layer),
            _layer_spec((D_MODEL, D_MODEL), layer),
        ],
        out_specs=[pl.BlockSpec((tm, D_MODEL), lambda i: (i, 0)),
                   pl.BlockSpec((tm, D_MODEL), lambda i: (i, 0))],
        out_shape=[jax.ShapeDtypeStruct((t, D_MODEL), F32),
                   jax.ShapeDtypeStruct((t, D_MODEL), BF16)],
        scratch_shapes=[pltpu.VMEM((tm, D_MODEL), BF16),
                        pltpu.VMEM((S5_Q, tm, FNET_GROUP), F32)],
        compiler_params=_cparams("parallel"),
        name="merge",
    )(x, h, fo, sg, y4, g_ffn, w_in, w_in, w_in, wf, ws, wglu, wo)


def _ffn_kernel(hp_ref, hm_ref, hn_ref, x_ref, wup_ref, cw_ref, wdn_ref, gf_ref,
                o_ref, h_ref, up_ref, act_ref, *, tm, seq, final):
    pos = lax.rem(pl.program_id(0) * tm, seq)
    zero = jnp.zeros((HALO, D_MODEL), BF16)
    h_ref[0:HALO] = jnp.where(pos != 0, hp_ref[...], zero)
    h_ref[HALO:HALO + tm] = hm_ref[...]
    h_ref[HALO + tm:] = jnp.where(pos + tm != seq, hn_ref[...], zero)
    hext = h_ref[...]
    rows = tm + 2 * HALO
    for j in range(FF_NBLK):
        gate_cols = slice(j * FF_BLK, (j + 1) * FF_BLK)
        val_cols = slice(D_FF + j * FF_BLK, D_FF + (j + 1) * FF_BLK)
        up = up_ref.at[j % 2]
        up[:, :FF_BLK] = _dot(hext, wup_ref[:, gate_cols])
        up[:, FF_BLK:] = _dot(hext, wup_ref[:, val_cols])
        cw = jnp.concatenate([cw_ref[:, gate_cols], cw_ref[:, val_cols]], axis=1)
        u = up[...]
        conv = (pltpu.roll(u, 1, 0) * cw[0:1] + u * cw[1:2]
                + pltpu.roll(u, rows - 1, 0) * cw[2:3] + cw[3:4])[HALO:HALO + tm]
        gate = conv[:, :FF_BLK]
        half = 0.5 * gate
        act = (half + half * jnp.tanh(half)) * conv[:, FF_BLK:]
        act_ref[:, j * FF_BLK:(j + 1) * FF_BLK] = act.astype(BF16)
    y = x_ref[...] + _dot(act_ref[...], wdn_ref[...])
    if final:
        y = _rms(y, gf_ref[...])
    o_ref[...] = y


def _ffn(x, h, layer, wup, cw, wdn, gf, seq, final):
    t = x.shape[0]
    tm = FFN_TILE
    per = tm // HALO
    last = t // HALO - 1
    rows = tm + 2 * HALO
    return pl.pallas_call(
        functools.partial(_ffn_kernel, tm=tm, seq=seq, final=final),
        grid=(t // tm,),
        in_specs=[
            pl.BlockSpec((HALO, D_MODEL), lambda i: (jnp.maximum(i * per - 1, 0), 0)),
            pl.BlockSpec((tm, D_MODEL), lambda i: (i, 0)),
            pl.BlockSpec((HALO, D_MODEL), lambda i: (jnp.minimum((i + 1) * per, last), 0)),
            pl.BlockSpec((tm, D_MODEL), lambda i: (i, 0)),
            _layer_spec((D_MODEL, 2 * D_FF), layer, resident=True),
            _layer_spec((8, 2 * D_FF), layer),
            _layer_spec((D_FF, D_MODEL), layer, resident=True),
            _const_spec((1, D_MODEL)),
        ],
        out_specs=pl.BlockSpec((tm, D_MODEL), lambda i: (i, 0)),
        out_shape=jax.ShapeDtypeStruct((t, D_MODEL), F32),
        scratch_shapes=[pltpu.VMEM((rows, D_MODEL), BF16),
                        pltpu.VMEM((2, rows, 2 * FF_BLK), F32),
                        pltpu.VMEM((tm, D_FF), BF16)],
        compiler_params=_cparams("parallel"),
        name="ffn_final" if final else "ffn",
    )(h, h, h, x, wup, cw, wdn, gf)


def _dft_consts(n1):
    n2 = FFT_N2
    s = n1 * n2
    k = np.arange(FNET_GROUP)
    ang = 2 * np.pi * ((k[:, None] * k[None, :]) % FNET_GROUP) / FNET_GROUP
    fc = np.concatenate([np.cos(ang), -np.sin(ang)], axis=1) / math.sqrt(FNET_GROUP)
    k1 = np.arange(n1)[None, :, None]
    s1 = np.arange(n1)[None, None, :]
    s2 = np.arange(n2)[:, None, None]
    ang = 2 * np.pi * ((k1 * (n2 * s1 + s2)) % s) / s
    c, sn = np.cos(ang), np.sin(ang)
    f1 = np.concatenate([np.concatenate([c, sn], axis=2),
                         np.concatenate([-sn, c], axis=2)], axis=1) / math.sqrt(n1)
    k2 = np.arange(n2)
    ang = 2 * np.pi * ((k2[:, None] * k2[None, :]) % n2) / n2
    f2 = np.concatenate([np.cos(ang), np.sin(ang)], axis=1) / math.sqrt(n2)
    return (jnp.asarray(fc, BF16), jnp.asarray(f1, BF16), jnp.asarray(f2, BF16))


def _cmul(ar, ai, br, bi):
    return ar * br - ai * bi, ar * bi + ai * br


def _s5_operators(lam_re, lam_im, log_dt, b_re, b_im, c_re, c_im, d_skip):
    L, G, P, H, Q, GQ = S5_L, S5_G, S5_P, S5_H, S5_Q, S5_GQ
    dt = jnp.exp(log_dt)[..., None]
    lr, li = lam_re * dt, lam_im * dt

    def lb_pow(n):
        n = jnp.asarray(n, F32)[:, None, None, None]
        mag = jnp.exp(n * lr)
        return mag * jnp.cos(n * li), mag * jnp.sin(n * li)

    lbr, lbi = lb_pow([1.0])
    den = lam_re * lam_re + lam_im * lam_im
    bsr, bsi = _cmul(lbr[0] - 1.0, lbi[0], lam_re / den, -lam_im / den)
    cbr, cbi = _cmul(c_re, c_im, bsr[:, :, None, :], bsi[:, :, None, :])

    pw_r, pw_i = lb_pow(np.arange(L + 1))

    def lag_kernel(d):
        wr, wi = _cmul(cbr[d][None], cbi[d][None],
                       pw_r[:L, d][:, :, None, :], pw_i[:L, d][:, :, None, :])
        return (jnp.einsum('sghp,gpk->sghk', wr, b_re)
                - jnp.einsum('sghp,gpk->sghk', wi, b_im))

    kf, kb = lag_kernel(0), lag_kernel(1)
    klag = jnp.concatenate([kb[1:][::-1], (kf[0] + kb[0])[None], kf[1:]], axis=0)
    idx = np.arange(L)[None, :] - np.arange(L)[:, None] + (L - 1)
    kss = klag[idx]
    gw = L * H
    toep = jnp.transpose(kss, (2, 0, 4, 1, 3)).reshape(G, gw, gw)
    pair_eye = jnp.eye(2, dtype=F32)

    def end_state(d, exps):
        er, ei = pw_r[exps, d], pw_i[exps, d]
        return _cmul(er[..., None], ei[..., None], b_re[None], b_im[None])

    fr, fi = end_state(0, np.arange(L)[::-1])
    br_, bi_ = end_state(1, np.arange(L))
    parts = jnp.stack([fr, fi, br_, bi_], axis=0)
    parts = jnp.transpose(parts, (2, 1, 4, 0, 3)).reshape(G // 2, 2, gw, 4, P)
    mend = jnp.einsum('ngrcp,gk->ngrckp', parts, pair_eye).reshape(G // 2, 2 * gw, 4 * 2 * P)

    def read_out(d, exps):
        wr, wi = _cmul(cbr[d][None], cbi[d][None],
                       pw_r[exps, d][:, :, None, :], pw_i[exps, d][:, :, None, :])
        return wr, -wi

    rfr, rfi = read_out(0, np.arange(1, L + 1))
    rbr, rbi = read_out(1, np.arange(L, 0, -1))
    parts = jnp.stack([rfr, rfi, rbr, rbi], axis=0)
    parts = jnp.transpose(parts, (2, 0, 4, 1, 3)).reshape(G // 2, 2, 4, P, gw)
    rout = jnp.einsum('ngcpr,gk->ncgpkr', parts, pair_eye).reshape(G // 2, 4 * 2 * P, 2 * gw)

    ar = pw_r[L].reshape(2, Q, GQ * P // FNET_GROUP, FNET_GROUP)
    ai = pw_i[L].reshape(2, Q, GQ * P // FNET_GROUP, FNET_GROUP)
    a_re = jnp.concatenate([ar, ar], axis=2)
    a_im = jnp.concatenate([-ai, ai], axis=2)
    decay = jnp.stack([a_re[0], a_im[0], a_re[1], a_im[1]], axis=1)

    dtile = jnp.broadcast_to(d_skip[:, None, :], (G, L, H)).reshape(Q, 1, S5_W)
    return mend.astype(BF16), decay, toep.astype(BF16), rout.astype(BF16), dtile


def _pack_sgu(sgu_w, sgu_b):
    ws = jnp.concatenate([sgu_w[0::2], sgu_w[1::2]], axis=2).astype(BF16)
    sb = jnp.repeat(sgu_b.T, SGU_HEAD_DIM, axis=1)
    return ws, sb


def _trunk(x, p, consts):
    b, seq, _ = x.shape
    n1 = seq // FFT_N2
    nc = seq // S5_L
    t = b * seq
    fc, f1, f2 = consts
    depth = p["w_in"].shape[0]
    x = x.reshape(t, D_MODEL)
    for li in range(depth):
        h, xf, sg, c4 = _proj(x, li, p["g_mix"], p["w_in"], fc, p["v_gain"], p["sgu_w"],
                              p["sgu_b"], b, n1)
        fo = _fft2(_fft1(xf, f1, b, n1), f2, b, n1).reshape(t, BRANCH)
        y4 = _s5(c4, li, p["decay"], p["mend"], p["toep"], p["rout"], p["dtile"], b, nc)
        x, h2 = _merge(x, h, fo, sg, y4, li, p["g_ffn"], p["w_in"], p["w_fnet"], p["w_sgu"],
                       p["w_glu"], p["w_out"])
        x = _ffn(x, h2, li, p["w_up"], p["conv"], p["w_down"], p["g_final"], seq,
                 final=(li == depth - 1))
    return x.reshape(b, seq, D_MODEL)


def kernel(x_prompt, x_sample, g_mix, w_in, w_fnet, sgu_v_gain, sgu_w, sgu_b, w_sgu,
           s5_lam_re, s5_lam_im, s5_log_dt, s5_b_re, s5_b_im, s5_c_re, s5_c_im, s5_d,
           w_glu, w_out, g_ffn, w_up, conv_w, conv_b, w_down, g_final):
    depth = w_in.shape[0]
    ws, sb = jax.vmap(_pack_sgu)(sgu_w, sgu_b)
    mend, decay, toep, rout, dtile = jax.vmap(_s5_operators)(
        s5_lam_re, s5_lam_im, s5_log_dt, s5_b_re, s5_b_im, s5_c_re, s5_c_im, s5_d)
    conv = jnp.concatenate([conv_w, conv_b[:, None], jnp.zeros((depth, 4, 2 * D_FF), F32)], axis=1)
    params = dict(
        g_mix=g_mix[:, None], w_in=w_in.astype(BF16), v_gain=sgu_v_gain[:, None],
        sgu_w=ws, sgu_b=sb, w_fnet=w_fnet.astype(BF16), w_sgu=w_sgu.astype(BF16),
        w_glu=w_glu.astype(BF16), w_out=w_out.astype(BF16),
        mend=mend, decay=decay, toep=toep, rout=rout, dtile=dtile,
        g_ffn=g_ffn[:, None], w_up=w_up.astype(BF16), conv=conv, w_down=w_down.astype(BF16),
        g_final=g_final[None])
    outs = []
    for x in (x_prompt, x_sample):
        outs.append(_trunk(x, params, _dft_consts(x.shape[1] // FFT_N2)))
    return tuple(outs)
```

```python
import functools
import math

import numpy as np
import jax
import jax.numpy as jnp
from jax import lax
from jax.experimental import pallas as pl
from jax.experimental.pallas import tpu as pltpu

F32 = jnp.float32
BF16 = jnp.bfloat16

D_MODEL = 1024
BRANCH = 512
FNET_GROUP = 128
FNET_GROUPS = BRANCH // FNET_GROUP
SGU_CHUNK = 128
SGU_HEADS = 8
SGU_HEAD_DIM = BRANCH // SGU_HEADS
S5_H = 16
S5_G = BRANCH // S5_H
S5_P = 64
S5_L = 16
S5_Q = 4
S5_GQ = S5_G // S5_Q
S5_W = S5_L * FNET_GROUP
S5_PIECES = S5_W // FNET_GROUP
S5_PAIRS = S5_GQ // 2
S5_PW = 2 * S5_L * S5_H
D_FF = 2816
FF_BLK = 256
FF_NBLK = D_FF // FF_BLK
RMS_EPS = 1e-6
FFT_N2 = 128
HALO = 16

TOKEN_TILE = 512
PROJ_TILE = 1024
FFN_TILE = 1024
VMEM_LIMIT = 60 * 1024 * 1024


def _cparams(*sem):
    return pltpu.CompilerParams(dimension_semantics=sem, vmem_limit_bytes=VMEM_LIMIT)


def _const_spec(shape):
    nd = len(shape)
    return pl.BlockSpec(shape, lambda *_: (0,) * nd)


def _layer_spec(shape, layer, index=None, resident=False):
    idx = (layer,) + ((0,) * len(shape) if index is None else tuple(index))
    kwargs = dict(pipeline_mode=pl.Buffered(1)) if resident else {}
    return pl.BlockSpec((None,) + tuple(shape), lambda *_: idx, **kwargs)


def _dot(a, b):
    return jnp.dot(a, b, preferred_element_type=F32)


def _rms(x, g):
    return x * lax.rsqrt(jnp.mean(x * x, axis=-1, keepdims=True) + RMS_EPS) * g


def _pack_complex(re, im):
    return pltpu.pack_elementwise([re, im], packed_dtype=BF16)


def _unpack_complex(w):
    re = pltpu.unpack_elementwise(w, index=0, packed_dtype=BF16, unpacked_dtype=F32)
    im = pltpu.unpack_elementwise(w, index=1, packed_dtype=BF16, unpacked_dtype=F32)
    return jnp.concatenate([re, im], axis=0).astype(BF16)


def _granule_transpose(arrs):
    lane = lax.broadcasted_iota(jnp.int32, arrs[0].shape, 1)
    a = list(arrs)
    for d in (4, 2, 1):
        upper = ((lane // S5_H) & d) != 0
        for i in range(S5_GQ):
            if i & d:
                continue
            lo, hi = a[i], a[i + d]
            a[i] = jnp.where(upper, pltpu.roll(hi, d * S5_H, 1), lo)
            a[i + d] = jnp.where(upper, hi, pltpu.roll(lo, FNET_GROUP - d * S5_H, 1))
    return a


def _proj_kernel(x_ref, g_ref, w_ref, fc_ref, vg_ref, ws_ref, sb_ref,
                 h_ref, xf_ref, sg_ref, c4_ref, c_scr, *, tm):
    h = _rms(x_ref[...], g_ref[...]).astype(BF16)
    h_ref[...] = h

    a = _dot(h, w_ref[:, 0:BRANCH]).astype(BF16)
    for g in range(FNET_GROUPS):
        lo, hi = g * FNET_GROUP, (g + 1) * FNET_GROUP
        xc = _dot(a[:, lo:hi], fc_ref[...])
        packed = _pack_complex(xc[:, :FNET_GROUP], xc[:, FNET_GROUP:])
        for j in range(tm // FFT_N2):
            xf_ref[0, :, j, lo:hi] = packed[j * FFT_N2:(j + 1) * FFT_N2]

    c = _dot(h, w_ref[:, 3 * BRANCH:4 * BRANCH])
    for q in range(S5_Q):
        c_scr[q] = c[:, q * FNET_GROUP:(q + 1) * FNET_GROUP]
    for q in range(S5_Q):
        for half in range(2):
            by_time = [c_scr[q, pl.ds(S5_GQ * half + i, tm // S5_L, stride=S5_L), :]
                       for i in range(S5_GQ)]
            for gl, blk in enumerate(_granule_transpose(by_time)):
                lo = gl * 2 * FNET_GROUP + half * FNET_GROUP
                c4_ref[q, :, lo:lo + FNET_GROUP] = blk.astype(BF16)

    u = jax.nn.gelu(_dot(h, w_ref[:, BRANCH:2 * BRANCH]))
    v = jax.nn.gelu(_dot(h, w_ref[:, 2 * BRANCH:3 * BRANCH]))
    v = (_rms(v, vg_ref[...])).astype(BF16)
    lane = lax.broadcasted_iota(jnp.int32, (SGU_CHUNK, FNET_GROUP), 1)
    first_head = lane < SGU_HEAD_DIM
    zero = jnp.zeros((SGU_CHUNK, FNET_GROUP), BF16)
    for ck in range(tm // SGU_CHUNK):
        rows = slice(ck * SGU_CHUNK, (ck + 1) * SGU_CHUNK)
        for j in range(SGU_HEADS // 2):
            cols = slice(j * FNET_GROUP, (j + 1) * FNET_GROUP)
            blk = v[rows, cols]
            rhs = jnp.concatenate([jnp.where(first_head, blk, zero),
                                   jnp.where(first_head, zero, blk)], axis=0)
            mixed = _dot(ws_ref[j], rhs) + sb_ref[:, cols]
            sg_ref[rows, cols] = (u[rows, cols] * mixed).astype(BF16)


def _proj(x, layer, g, w, fc, vg, ws, sb, b, n1):
    t = x.shape[0]
    tm = PROJ_TILE
    s1_blk = tm // FFT_N2
    per_seq = n1 // s1_blk
    return pl.pallas_call(
        functools.partial(_proj_kernel, tm=tm),
        grid=(t // tm,),
        in_specs=[
            pl.BlockSpec((tm, D_MODEL), lambda i: (i, 0)),
            _layer_spec((1, D_MODEL), layer),
            _layer_spec((D_MODEL, 4 * BRANCH), layer, resident=True),
            _const_spec((FNET_GROUP, 2 * FNET_GROUP)),
            _layer_spec((1, BRANCH), layer),
            _layer_spec((SGU_HEADS // 2, SGU_CHUNK, 2 * SGU_CHUNK), layer),
            _layer_spec((SGU_CHUNK, BRANCH), layer),
        ],
        out_specs=[
            pl.BlockSpec((tm, D_MODEL), lambda i: (i, 0)),
            pl.BlockSpec((1, FFT_N2, s1_blk, BRANCH), lambda i: (i // per_seq, 0, i % per_seq, 0)),
            pl.BlockSpec((tm, BRANCH), lambda i: (i, 0)),
            pl.BlockSpec((S5_Q, tm // S5_L, S5_W), lambda i: (0, i, 0)),
        ],
        out_shape=[
            jax.ShapeDtypeStruct((t, D_MODEL), BF16),
            jax.ShapeDtypeStruct((b, FFT_N2, n1, BRANCH), jnp.uint32),
            jax.ShapeDtypeStruct((t, BRANCH), BF16),
            jax.ShapeDtypeStruct((S5_Q, t // S5_L, S5_W), BF16),
        ],
        scratch_shapes=[pltpu.VMEM((S5_Q, tm, FNET_GROUP), F32)],
        compiler_params=_cparams("parallel"),
        name="proj",
    )(x, g, w, fc, vg, ws, sb)


def _fft1_kernel(x_ref, f_ref, z_ref, *, n1, sb):
    for j in range(sb):
        z = _dot(f_ref[j], _unpack_complex(x_ref[0, j]))
        z_ref[0, :, j, :] = _pack_complex(z[:n1], z[n1:])


def _fft1(xf, f1, b, n1):
    sb = 8
    return pl.pallas_call(
        functools.partial(_fft1_kernel, n1=n1, sb=sb),
        grid=(b, FFT_N2 // sb),
        in_specs=[
            pl.BlockSpec((1, sb, n1, BRANCH), lambda i, j: (i, j, 0, 0)),
            pl.BlockSpec((sb, 2 * n1, 2 * n1), lambda i, j: (j, 0, 0)),
        ],
        out_specs=pl.BlockSpec((1, n1, sb, BRANCH), lambda i, j: (i, 0, j, 0)),
        out_shape=jax.ShapeDtypeStruct((b, n1, FFT_N2, BRANCH), jnp.uint32),
        compiler_params=_cparams("parallel", "parallel"),
        name="fft1",
    )(xf, f1)


def _fft2_kernel(z_ref, f_ref, o_ref, *, kb):
    for k in range(kb):
        o_ref[0, :, k, :] = _dot(f_ref[...], _unpack_complex(z_ref[0, k]))


def _fft2(z, f2, b, n1):
    kb = 8
    return pl.pallas_call(
        functools.partial(_fft2_kernel, kb=kb),
        grid=(b, n1 // kb),
        in_specs=[
            pl.BlockSpec((1, kb, FFT_N2, BRANCH), lambda i, j: (i, j, 0, 0)),
            _const_spec((FFT_N2, 2 * FFT_N2)),
        ],
        out_specs=pl.BlockSpec((1, FFT_N2, kb, BRANCH), lambda i, j: (i, 0, j, 0)),
        out_shape=jax.ShapeDtypeStruct((b, FFT_N2, n1, BRANCH), F32),
        compiler_params=_cparams("parallel", "parallel"),
        name="fft2",
    )(z, f2)


def _s5_kernel(x_ref, a_ref, m_ref, t_ref, r_ref, d_ref, y_ref, st_ref, *, nc):
    half = S5_PIECES // 2
    gw = S5_PW // 2

    for pp in range(S5_PAIRS):
        e = _dot(x_ref[0, :, pp * S5_PW:(pp + 1) * S5_PW], m_ref[pp])
        for k in range(4):
            st_ref[pl.ds(S5_PAIRS * k + pp, nc, stride=S5_PIECES), :] = (
                e[:, k * FNET_GROUP:(k + 1) * FNET_GROUP])

    arf, aif, arb, aib = a_ref[0, 0], a_ref[0, 1], a_ref[0, 2], a_ref[0, 3]

    def body(n, carry):
        hf, hb = carry
        fwd = pl.ds(pl.multiple_of(n * S5_PIECES, S5_PIECES), half)
        bwd = pl.ds(pl.multiple_of((nc - 1 - n) * S5_PIECES + half, half), half)
        ef = st_ref[fwd, :]
        eb = st_ref[bwd, :]
        st_ref[fwd, :] = hf
        st_ref[bwd, :] = hb
        hf = arf * hf + aif * pltpu.roll(hf, half // 2, 0) + ef
        hb = arb * hb + aib * pltpu.roll(hb, half // 2, 0) + eb
        return hf, hb

    zero = jnp.zeros((half, FNET_GROUP), F32)
    lax.fori_loop(0, nc, body, (zero, zero), unroll=4)

    for pp in range(S5_PAIRS):
        cols = slice(pp * S5_PW, (pp + 1) * S5_PW)
        x = x_ref[0, :, cols]
        h = jnp.concatenate([st_ref[pl.ds(S5_PAIRS * k + pp, nc, stride=S5_PIECES), :]
                             for k in range(4)], axis=1)
        y = jnp.concatenate([_dot(x[:, :gw], t_ref[2 * pp]),
                             _dot(x[:, gw:], t_ref[2 * pp + 1])], axis=1)
        y = y + _dot(h.astype(BF16), r_ref[pp]) + x.astype(F32) * d_ref[0, :, cols]
        y_ref[0, :, cols] = y.astype(BF16)


def _s5(c4, layer, decay, mend, toep, rout, dskip, b, nc):
    return pl.pallas_call(
        functools.partial(_s5_kernel, nc=nc),
        grid=(S5_Q, b),
        in_specs=[
            pl.BlockSpec((1, nc, S5_W), lambda q, i: (q, i, 0)),
            pl.BlockSpec((None, 1, 4, S5_PIECES // 2, FNET_GROUP), lambda q, i: (layer, q, 0, 0, 0)),
            pl.BlockSpec((None, S5_PAIRS, S5_PW, S5_PW), lambda q, i: (layer, q, 0, 0)),
            pl.BlockSpec((None, S5_GQ, S5_PW // 2, S5_PW // 2), lambda q, i: (layer, q, 0, 0)),
            pl.BlockSpec((None, S5_PAIRS, S5_PW, S5_PW), lambda q, i: (layer, q, 0, 0)),
            pl.BlockSpec((None, 1, 1, S5_W), lambda q, i: (layer, q, 0, 0)),
        ],
        out_specs=pl.BlockSpec((1, nc, S5_W), lambda q, i: (q, i, 0)),
        out_shape=jax.ShapeDtypeStruct((S5_Q, b * nc, S5_W), BF16),
        scratch_shapes=[pltpu.VMEM((nc * S5_PIECES, FNET_GROUP), F32)],
        compiler_params=_cparams("parallel", "parallel"),
        name="s5",
    )(c4, decay, mend, toep, rout, dskip)


def _merge_kernel(x_ref, h_ref, fo_ref, sg_ref, y4_ref, g_ref, wga_ref, wgb_ref, wgc_ref,
                  wf_ref, ws_ref, wglu_ref, wo_ref, o_ref, h2_ref, m_ref, y_scr, *, tm):
    h = h_ref[...]
    fo = fo_ref[...].astype(BF16)
    sg = sg_ref[...]
    for q in range(S5_Q):
        for half in range(2):
            by_group = [y4_ref[q, :, (2 * gl + half) * FNET_GROUP:
                               (2 * gl + half + 1) * FNET_GROUP].astype(F32)
                        for gl in range(S5_GQ)]
            for i, blk in enumerate(_granule_transpose(by_group)):
                y_scr[q, pl.ds(S5_GQ * half + i, tm // S5_L, stride=S5_L), :] = blk
    ys = jax.nn.gelu(jnp.concatenate([y_scr[q] for q in range(S5_Q)], axis=1)).astype(BF16)
    nb = 256
    for c0 in range(0, D_MODEL, nb):
        cs = slice(c0, c0 + nb)
        ga = jax.nn.sigmoid(_dot(h, wga_ref[:, cs]))
        gb = jax.nn.sigmoid(_dot(h, wgb_ref[:, cs]))
        gc = jax.nn.sigmoid(_dot(h, wgc_ref[:, cs]))
        ya = _dot(fo, wf_ref[:, cs])
        yb = _dot(sg, ws_ref[:, cs])
        val = _dot(ys, wglu_ref[:, cs])
        gate = _dot(ys, wglu_ref[:, D_MODEL + c0:D_MODEL + c0 + nb])
        yc = val * jax.nn.sigmoid(gate)
        m_ref[:, cs] = (ga * ya + gb * yb + gc * yc).astype(BF16)
    g = g_ref[...]
    for r0 in range(0, tm, SGU_CHUNK):
        rows = slice(r0, r0 + SGU_CHUNK)
        xo = x_ref[rows, :] + _dot(m_ref[rows, :], wo_ref[...])
        o_ref[rows, :] = xo
        h2_ref[rows, :] = _rms(xo, g).astype(BF16)


def _merge(x, h, fo, sg, y4, layer, g_ffn, w_in, wf, ws, wglu, wo):
    gate0 = 4 * BRANCH // D_MODEL
    t = x.shape[0]
    tm = TOKEN_TILE
    return pl.pallas_call(
        functools.partial(_merge_kernel, tm=tm),
        grid=(t // tm,),
        in_specs=[
            pl.BlockSpec((tm, D_MODEL), lambda i: (i, 0)),
            pl.BlockSpec((tm, D_MODEL), lambda i: (i, 0)),
            pl.BlockSpec((tm, BRANCH), lambda i: (i, 0)),
            pl.BlockSpec((tm, BRANCH), lambda i: (i, 0)),
            pl.BlockSpec((S5_Q, tm // S5_L, S5_W), lambda i: (0, i, 0)),
            _layer_spec((1, D_MODEL), layer),
            _layer_spec((D_MODEL, D_MODEL), layer, index=(0, gate0)),
            _layer_spec((D_MODEL, D_MODEL), layer, index=(0, gate0 + 1)),
            _layer_spec((D_MODEL, D_MODEL), layer, index=(0, gate0 + 2)),
            _layer_spec((BRANCH, D_MODEL), layer),
            _layer_spec((BRANCH, D_MODEL), layer),
            _layer_spec((BRANCH, 2 * D_MODEL), layer),
            _layer_spec((D_MODEL, D_MODEL), layer),
        ],
        out_specs=[pl.BlockSpec((tm, D_MODEL), lambda i: (i, 0)),
                   pl.BlockSpec((tm, D_MODEL), lambda i: (i, 0))],
        out_shape=[jax.ShapeDtypeStruct((t, D_MODEL), F32),
                   jax.ShapeDtypeStruct((t, D_MODEL), BF16)],
        scratch_shapes=[pltpu.VMEM((tm, D_MODEL), BF16),
                        pltpu.VMEM((S5_Q, tm, FNET_GROUP), F32)],
        compiler_params=_cparams("parallel"),
        name="merge",
    )(x, h, fo, sg, y4, g_ffn, w_in, w_in, w_in, wf, ws, wglu, wo)


def _ffn_kernel(hp_ref, hm_ref, hn_ref, x_ref, wup_ref, cw_ref, wdn_ref, gf_ref,
                o_ref, h_ref, up_ref, act_ref, *, tm, seq, final):
    pos = lax.rem(pl.program_id(0) * tm, seq)
    zero = jnp.zeros((HALO, D_MODEL), BF16)
    h_ref[0:HALO] = jnp.where(pos != 0, hp_ref[...], zero)
    h_ref[HALO:HALO + tm] = hm_ref[...]
    h_ref[HALO + tm:] = jnp.where(pos + tm != seq, hn_ref[...], zero)
    hext = h_ref[...]
    rows = tm + 2 * HALO
    for j in range(FF_NBLK):
        gate_cols = slice(j * FF_BLK, (j + 1) * FF_BLK)
        val_cols = slice(D_FF + j * FF_BLK, D_FF + (j + 1) * FF_BLK)
        up = up_ref.at[j % 2]
        up[:, :FF_BLK] = _dot(hext, wup_ref[:, gate_cols])
        up[:, FF_BLK:] = _dot(hext, wup_ref[:, val_cols])
        cw = jnp.concatenate([cw_ref[:, gate_cols], cw_ref[:, val_cols]], axis=1)
        u = up[...]
        conv = (pltpu.roll(u, 1, 0) * cw[0:1] + u * cw[1:2]
                + pltpu.roll(u, rows - 1, 0) * cw[2:3] + cw[3:4])[HALO:HALO + tm]
        gate = conv[:, :FF_BLK]
        half = 0.5 * gate
        act = (half + half * jnp.tanh(half)) * conv[:, FF_BLK:]
        act_ref[:, j * FF_BLK:(j + 1) * FF_BLK] = act.astype(BF16)
    y = x_ref[...] + _dot(act_ref[...], wdn_ref[...])
    if final:
        y = _rms(y, gf_ref[...])
    o_ref[...] = y


def _ffn(x, h, layer, wup, cw, wdn, gf, seq, final):
    t = x.shape[0]
    tm = FFN_TILE
    per = tm // HALO
    last = t // HALO - 1
    rows = tm + 2 * HALO
    return pl.pallas_call(
        functools.partial(_ffn_kernel, tm=tm, seq=seq, final=final),
        grid=(t // tm,),
        in_specs=[
            pl.BlockSpec((HALO, D_MODEL), lambda i: (jnp.maximum(i * per - 1, 0), 0)),
            pl.BlockSpec((tm, D_MODEL), lambda i: (i, 0)),
            pl.BlockSpec((HALO, D_MODEL), lambda i: (jnp.minimum((i + 1) * per, last), 0)),
            pl.BlockSpec((tm, D_MODEL), lambda i: (i, 0)),
            _layer_spec((D_MODEL, 2 * D_FF), layer, resident=True),
            _layer_spec((8, 2 * D_FF), layer),
            _layer_spec((D_FF, D_MODEL), layer, resident=True),
            _const_spec((1, D_MODEL)),
        ],
        out_specs=pl.BlockSpec((tm, D_MODEL), lambda i: (i, 0)),
        out_shape=jax.ShapeDtypeStruct((t, D_MODEL), F32),
        scratch_shapes=[pltpu.VMEM((rows, D_MODEL), BF16),
                        pltpu.VMEM((2, rows, 2 * FF_BLK), F32),
                        pltpu.VMEM((tm, D_FF), BF16)],
        compiler_params=_cparams("parallel"),
        name="ffn_final" if final else "ffn",
    )(h, h, h, x, wup, cw, wdn, gf)


def _dft_consts(n1):
    n2 = FFT_N2
    s = n1 * n2
    k = np.arange(FNET_GROUP)
    ang = 2 * np.pi * ((k[:, None] * k[None, :]) % FNET_GROUP) / FNET_GROUP
    fc = np.concatenate([np.cos(ang), -np.sin(ang)], axis=1) / math.sqrt(FNET_GROUP)
    k1 = np.arange(n1)[None, :, None]
    s1 = np.arange(n1)[None, None, :]
    s2 = np.arange(n2)[:, None, None]
    ang = 2 * np.pi * ((k1 * (n2 * s1 + s2)) % s) / s
    c, sn = np.cos(ang), np.sin(ang)
    f1 = np.concatenate([np.concatenate([c, sn], axis=2),
                         np.concatenate([-sn, c], axis=2)], axis=1) / math.sqrt(n1)
    k2 = np.arange(n2)
    ang = 2 * np.pi * ((k2[:, None] * k2[None, :]) % n2) / n2
    f2 = np.concatenate([np.cos(ang), np.sin(ang)], axis=1) / math.sqrt(n2)
    return (jnp.asarray(fc, BF16), jnp.asarray(f1, BF16), jnp.asarray(f2, BF16))


def _cmul(ar, ai, br, bi):
    return ar * br - ai * bi, ar * bi + ai * br


def _s5_operators(lam_re, lam_im, log_dt, b_re, b_im, c_re, c_im, d_skip):
    L, G, P, H, Q, GQ = S5_L, S5_G, S5_P, S5_H, S5_Q, S5_GQ
    dt = jnp.exp(log_dt)[..., None]
    lr, li = lam_re * dt, lam_im * dt

    def lb_pow(n):
        n = jnp.asarray(n, F32)[:, None, None, None]
        mag = jnp.exp(n * lr)
        return mag * jnp.cos(n * li), mag * jnp.sin(n * li)

    lbr, lbi = lb_pow([1.0])
    den = lam_re * lam_re + lam_im * lam_im
    bsr, bsi = _cmul(lbr[0] - 1.0, lbi[0], lam_re / den, -lam_im / den)
    cbr, cbi = _cmul(c_re, c_im, bsr[:, :, None, :], bsi[:, :, None, :])

    pw_r, pw_i = lb_pow(np.arange(L + 1))

    def lag_kernel(d):
        wr, wi = _cmul(cbr[d][None], cbi[d][None],
                       pw_r[:L, d][:, :, None, :], pw_i[:L, d][:, :, None, :])
        return (jnp.einsum('sghp,gpk->sghk', wr, b_re)
                - jnp.einsum('sghp,gpk->sghk', wi, b_im))

    kf, kb = lag_kernel(0), lag_kernel(1)
    klag = jnp.concatenate([kb[1:][::-1], (kf[0] + kb[0])[None], kf[1:]], axis=0)
    idx = np.arange(L)[None, :] - np.arange(L)[:, None] + (L - 1)
    kss = klag[idx]
    gw = L * H
    toep = jnp.transpose(kss, (2, 0, 4, 1, 3)).reshape(G, gw, gw)
    pair_eye = jnp.eye(2, dtype=F32)

    def end_state(d, exps):
        er, ei = pw_r[exps, d], pw_i[exps, d]
        return _cmul(er[..., None], ei[..., None], b_re[None], b_im[None])

    fr, fi = end_state(0, np.arange(L)[::-1])
    br_, bi_ = end_state(1, np.arange(L))
    parts = jnp.stack([fr, fi, br_, bi_], axis=0)
    parts = jnp.transpose(parts, (2, 1, 4, 0, 3)).reshape(G // 2, 2, gw, 4, P)
    mend = jnp.einsum('ngrcp,gk->ngrckp', parts, pair_eye).reshape(G // 2, 2 * gw, 4 * 2 * P)

    def read_out(d, exps):
        wr, wi = _cmul(cbr[d][None], cbi[d][None],
                       pw_r[exps, d][:, :, None, :], pw_i[exps, d][:, :, None, :])
        return wr, -wi

    rfr, rfi = read_out(0, np.arange(1, L + 1))
    rbr, rbi = read_out(1, np.arange(L, 0, -1))
    parts = jnp.stack([rfr, rfi, rbr, rbi], axis=0)
    parts = jnp.transpose(parts, (2, 0, 4, 1, 3)).reshape(G // 2, 2, 4, P, gw)
    rout = jnp.einsum('ngcpr,gk->ncgpkr', parts, pair_eye).reshape(G // 2, 4 * 2 * P, 2 * gw)

    ar = pw_r[L].reshape(2, Q, GQ * P // FNET_GROUP, FNET_GROUP)
    ai = pw_i[L].reshape(2, Q, GQ * P // FNET_GROUP, FNET_GROUP)
    a_re = jnp.concatenate([ar, ar], axis=2)
    a_im = jnp.concatenate([-ai, ai], axis=2)
    decay = jnp.stack([a_re[0], a_im[0], a_re[1], a_im[1]], axis=1)

    dtile = jnp.broadcast_to(d_skip[:, None, :], (G, L, H)).reshape(Q, 1, S5_W)
    return mend.astype(BF16), decay, toep.astype(BF16), rout.astype(BF16), dtile


def _pack_sgu(sgu_w, sgu_b):
    ws = jnp.concatenate([sgu_w[0::2], sgu_w[1::2]], axis=2).astype(BF16)
    sb = jnp.repeat(sgu_b.T, SGU_HEAD_DIM, axis=1)
    return ws, sb


def _trunk(x, p, consts):
    b, seq, _ = x.shape
    n1 = seq // FFT_N2
    nc = seq // S5_L
    t = b * seq
    fc, f1, f2 = consts
    depth = p["w_in"].shape[0]
    x = x.reshape(t, D_MODEL)
    for li in range(depth):
        h, xf, sg, c4 = _proj(x, li, p["g_mix"], p["w_in"], fc, p["v_gain"], p["sgu_w"],
                              p["sgu_b"], b, n1)
        fo = _fft2(_fft1(xf, f1, b, n1), f2, b, n1).reshape(t, BRANCH)
        y4 = _s5(c4, li, p["decay"], p["mend"], p["toep"], p["rout"], p["dtile"], b, nc)
        x, h2 = _merge(x, h, fo, sg, y4, li, p["g_ffn"], p["w_in"], p["w_fnet"], p["w_sgu"],
                       p["w_glu"], p["w_out"])
        x = _ffn(x, h2, li, p["w_up"], p["conv"], p["w_down"], p["g_final"], seq,
                 final=(li == depth - 1))
    return x.reshape(b, seq, D_MODEL)


def kernel(x_prompt, x_sample, g_mix, w_in, w_fnet, sgu_v_gain, sgu_w, sgu_b, w_sgu,
           s5_lam_re, s5_lam_im, s5_log_dt, s5_b_re, s5_b_im, s5_c_re, s5_c_im, s5_d,
           w_glu, w_out, g_ffn, w_up, conv_w, conv_b, w_down, g_final):
    depth = w_in.shape[0]
    ws, sb = jax.vmap(_pack_sgu)(sgu_w, sgu_b)
    mend, decay, toep, rout, dtile = jax.vmap(_s5_operators)(
        s5_lam_re, s5_lam_im, s5_log_dt, s5_b_re, s5_b_im, s5_c_re, s5_c_im, s5_d)
    conv = jnp.concatenate([conv_w, conv_b[:, None], jnp.zeros((depth, 4, 2 * D_FF), F32)], axis=1)
    params = dict(
        g_mix=g_mix[:, None], w_in=w_in.astype(BF16), v_gain=sgu_v_gain[:, None],
        sgu_w=ws, sgu_b=sb, w_fnet=w_fnet.astype(BF16), w_sgu=w_sgu.astype(BF16),
        w_glu=w_glu.astype(BF16), w_out=w_out.astype(BF16),
        mend=mend, decay=decay, toep=toep, rout=rout, dtile=dtile,
        g_ffn=g_ffn[:, None], w_up=w_up.astype(BF16), conv=conv, w_down=w_down.astype(BF16),
        g_final=g_final[None])
    outs = []
    for x in (x_prompt, x_sample):
        outs.append(_trunk(x, params, _dft_consts(x.shape[1] // FFT_N2)))
    return tuple(outs)
```

```python
import functools
import math

import numpy as np
import jax
import jax.numpy as jnp
from jax import lax
from jax.experimental import pallas as pl
from jax.experimental.pallas import tpu as pltpu

F32 = jnp.float32
BF16 = jnp.bfloat16

D_MODEL = 1024
BRANCH = 512
FNET_GROUP = 128
FNET_GROUPS = BRANCH // FNET_GROUP
SGU_CHUNK = 128
SGU_HEADS = 8
SGU_HEAD_DIM = BRANCH // SGU_HEADS
S5_H = 16
S5_G = BRANCH // S5_H
S5_P = 64
S5_L = 16
S5_Q = 4
S5_GQ = S5_G // S5_Q
S5_W = S5_L * FNET_GROUP
S5_PIECES = S5_W // FNET_GROUP
S5_PAIRS = S5_GQ // 2
S5_PW = 2 * S5_L * S5_H
D_FF = 2816
FF_BLK = 256
FF_NBLK = D_FF // FF_BLK
RMS_EPS = 1e-6
FFT_N2 = 128
HALO = 16

TOKEN_TILE = 512
PROJ_TILE = 1024
FFN_TILE = 1024
VMEM_LIMIT = 60 * 1024 * 1024


def _cparams(*sem):
    return pltpu.CompilerParams(dimension_semantics=sem, vmem_limit_bytes=VMEM_LIMIT)


def _const_spec(shape):
    nd = len(shape)
    return pl.BlockSpec(shape, lambda *_: (0,) * nd)


def _layer_spec(shape, layer, index=None, resident=False):
    idx = (layer,) + ((0,) * len(shape) if index is None else tuple(index))
    kwargs = dict(pipeline_mode=pl.Buffered(1)) if resident else {}
    return pl.BlockSpec((None,) + tuple(shape), lambda *_: idx, **kwargs)


def _dot(a, b):
    return jnp.dot(a, b, preferred_element_type=F32)


def _rms(x, g):
    return x * lax.rsqrt(jnp.mean(x * x, axis=-1, keepdims=True) + RMS_EPS) * g


def _pack_complex(re, im):
    return pltpu.pack_elementwise([re, im], packed_dtype=BF16)


def _unpack_complex(w):
    re = pltpu.unpack_elementwise(w, index=0, packed_dtype=BF16, unpacked_dtype=F32)
    im = pltpu.unpack_elementwise(w, index=1, packed_dtype=BF16, unpacked_dtype=F32)
    return jnp.concatenate([re, im], axis=0).astype(BF16)


def _granule_transpose(arrs):
    lane = lax.broadcasted_iota(jnp.int32, arrs[0].shape, 1)
    a = list(arrs)
    for d in (4, 2, 1):
        upper = ((lane // S5_H) & d) != 0
        for i in range(S5_GQ):
            if i & d:
                continue
            lo, hi = a[i], a[i + d]
            a[i] = jnp.where(upper, pltpu.roll(hi, d * S5_H, 1), lo)
            a[i + d] = jnp.where(upper, hi, pltpu.roll(lo, FNET_GROUP - d * S5_H, 1))
    return a


def _proj_kernel(x_ref, g_ref, w_ref, fc_ref, vg_ref, ws_ref, sb_ref,
                 h_ref, xf_ref, sg_ref, c4_ref, c_scr, *, tm):
    h = _rms(x_ref[...], g_ref[...]).astype(BF16)
    h_ref[...] = h

    v = _dot(h, w_ref[:, 2 * BRANCH:3 * BRANCH])
    u = _dot(h, w_ref[:, BRANCH:2 * BRANCH])
    c = _dot(h, w_ref[:, 3 * BRANCH:4 * BRANCH])
    a = _dot(h, w_ref[:, 0:BRANCH]).astype(BF16)

    v = (_rms(jax.nn.gelu(v), vg_ref[...])).astype(BF16)
    u = jax.nn.gelu(u)
    lane = lax.broadcasted_iota(jnp.int32, (SGU_CHUNK, FNET_GROUP), 1)
    first_head = lane < SGU_HEAD_DIM
    zero = jnp.zeros((SGU_CHUNK, FNET_GROUP), BF16)
    for ck in range(tm // SGU_CHUNK):
        rows = slice(ck * SGU_CHUNK, (ck + 1) * SGU_CHUNK)
        for j in range(SGU_HEADS // 2):
            cols = slice(j * FNET_GROUP, (j + 1) * FNET_GROUP)
            blk = v[rows, cols]
            rhs = jnp.concatenate([jnp.where(first_head, blk, zero),
                                   jnp.where(first_head, zero, blk)], axis=0)
            mixed = _dot(ws_ref[j], rhs) + sb_ref[:, cols]
            sg_ref[rows, cols] = (u[rows, cols] * mixed).astype(BF16)

    for q in range(S5_Q):
        c_scr[q] = c[:, q * FNET_GROUP:(q + 1) * FNET_GROUP]
    for q in range(S5_Q):
        for half in range(2):
            by_time = [c_scr[q, pl.ds(S5_GQ * half + i, tm // S5_L, stride=S5_L), :]
                       for i in range(S5_GQ)]
            for gl, blk in enumerate(_granule_transpose(by_time)):
                lo = gl * 2 * FNET_GROUP + half * FNET_GROUP
                c4_ref[q, :, lo:lo + FNET_GROUP] = blk.astype(BF16)

    for g in range(FNET_GROUPS):
        lo, hi = g * FNET_GROUP, (g + 1) * FNET_GROUP
        xc = _dot(a[:, lo:hi], fc_ref[...])
        packed = _pack_complex(xc[:, :FNET_GROUP], xc[:, FNET_GROUP:])
        for j in range(tm // FFT_N2):
            xf_ref[0, :, j, lo:hi] = packed[j * FFT_N2:(j + 1) * FFT_N2]


def _proj(x, layer, g, w, fc, vg, ws, sb, b, n1):
    t = x.shape[0]
    tm = PROJ_TILE
    s1_blk = tm // FFT_N2
    per_seq = n1 // s1_blk
    return pl.pallas_call(
        functools.partial(_proj_kernel, tm=tm),
        grid=(t // tm,),
        in_specs=[
            pl.BlockSpec((tm, D_MODEL), lambda i: (i, 0)),
            _layer_spec((1, D_MODEL), layer),
            _layer_spec((D_MODEL, 4 * BRANCH), layer, resident=True),
            _const_spec((FNET_GROUP, 2 * FNET_GROUP)),
            _layer_spec((1, BRANCH), layer),
            _layer_spec((SGU_HEADS // 2, SGU_CHUNK, 2 * SGU_CHUNK), layer),
            _layer_spec((SGU_CHUNK, BRANCH), layer),
        ],
        out_specs=[
            pl.BlockSpec((tm, D_MODEL), lambda i: (i, 0)),
            pl.BlockSpec((1, FFT_N2, s1_blk, BRANCH), lambda i: (i // per_seq, 0, i % per_seq, 0)),
            pl.BlockSpec((tm, BRANCH), lambda i: (i, 0)),
            pl.BlockSpec((S5_Q, tm // S5_L, S5_W), lambda i: (0, i, 0)),
        ],
        out_shape=[
            jax.ShapeDtypeStruct((t, D_MODEL), BF16),
            jax.ShapeDtypeStruct((b, FFT_N2, n1, BRANCH), jnp.uint32),
            jax.ShapeDtypeStruct((t, BRANCH), BF16),
            jax.ShapeDtypeStruct((S5_Q, t // S5_L, S5_W), BF16),
        ],
        scratch_shapes=[pltpu.VMEM((S5_Q, tm, FNET_GROUP), F32)],
        compiler_params=_cparams("parallel"),
        name="proj",
    )(x, g, w, fc, vg, ws, sb)


def _fft1_kernel(x_ref, f_ref, z_ref, *, n1, sb):
    for j in range(sb):
        z = _dot(f_ref[j], _unpack_complex(x_ref[0, j]))
        z_ref[0, :, j, :] = _pack_complex(z[:n1], z[n1:])


def _fft1(xf, f1, b, n1):
    sb = 16
    return pl.pallas_call(
        functools.partial(_fft1_kernel, n1=n1, sb=sb),
        grid=(b, FFT_N2 // sb),
        in_specs=[
            pl.BlockSpec((1, sb, n1, BRANCH), lambda i, j: (i, j, 0, 0)),
            pl.BlockSpec((sb, 2 * n1, 2 * n1), lambda i, j: (j, 0, 0)),
        ],
        out_specs=pl.BlockSpec((1, n1, sb, BRANCH), lambda i, j: (i, 0, j, 0)),
        out_shape=jax.ShapeDtypeStruct((b, n1, FFT_N2, BRANCH), jnp.uint32),
        compiler_params=_cparams("parallel", "parallel"),
        name="fft1",
    )(xf, f1)


def _fft2_kernel(z_ref, f_ref, o_ref, *, kb):
    for k in range(kb):
        o_ref[0, :, k, :] = _dot(f_ref[...], _unpack_complex(z_ref[0, k]))


def _fft2(z, f2, b, n1):
    kb = 16
    return pl.pallas_call(
        functools.partial(_fft2_kernel, kb=kb),
        grid=(b, n1 // kb),
        in_specs=[
            pl.BlockSpec((1, kb, FFT_N2, BRANCH), lambda i, j: (i, j, 0, 0)),
            _const_spec((FFT_N2, 2 * FFT_N2)),
        ],
        out_specs=pl.BlockSpec((1, FFT_N2, kb, BRANCH), lambda i, j: (i, 0, j, 0)),
        out_shape=jax.ShapeDtypeStruct((b, FFT_N2, n1, BRANCH), F32),
        compiler_params=_cparams("parallel", "parallel"),
        name="fft2",
    )(z, f2)


def _s5_kernel(x_ref, a_ref, m_ref, t_ref, r_ref, d_ref, y_ref, st_ref, *, nc):
    half = S5_PIECES // 2
    gw = S5_PW // 2

    for pp in range(S5_PAIRS):
        e = _dot(x_ref[0, :, pp * S5_PW:(pp + 1) * S5_PW], m_ref[pp])
        for k in range(4):
            st_ref[pl.ds(S5_PAIRS * k + pp, nc, stride=S5_PIECES), :] = (
                e[:, k * FNET_GROUP:(k + 1) * FNET_GROUP])

    arf, aif, arb, aib = a_ref[0, 0], a_ref[0, 1], a_ref[0, 2], a_ref[0, 3]

    def body(n, carry):
        hf, hb = carry
        fwd = pl.ds(pl.multiple_of(n * S5_PIECES, S5_PIECES), half)
        bwd = pl.ds(pl.multiple_of((nc - 1 - n) * S5_PIECES + half, half), half)
        ef = st_ref[fwd, :]
        eb = st_ref[bwd, :]
        st_ref[fwd, :] = hf
        st_ref[bwd, :] = hb
        hf = arf * hf + aif * pltpu.roll(hf, half // 2, 0) + ef
        hb = arb * hb + aib * pltpu.roll(hb, half // 2, 0) + eb
        return hf, hb

    zero = jnp.zeros((half, FNET_GROUP), F32)
    lax.fori_loop(0, nc, body, (zero, zero), unroll=4)

    for pp in range(S5_PAIRS):
        cols = slice(pp * S5_PW, (pp + 1) * S5_PW)
        x = x_ref[0, :, cols]
        h = jnp.concatenate([st_ref[pl.ds(S5_PAIRS * k + pp, nc, stride=S5_PIECES), :]
                             for k in range(4)], axis=1)
        y = jnp.concatenate([_dot(x[:, :gw], t_ref[2 * pp]),
                             _dot(x[:, gw:], t_ref[2 * pp + 1])], axis=1)
        y = y + _dot(h.astype(BF16), r_ref[pp]) + x.astype(F32) * d_ref[0, :, cols]
        y_ref[0, :, cols] = y.astype(BF16)


def _s5(c4, layer, decay, mend, toep, rout, dskip, b, nc):
    return pl.pallas_call(
        functools.partial(_s5_kernel, nc=nc),
        grid=(S5_Q, b),
        in_specs=[
            pl.BlockSpec((1, nc, S5_W), lambda q, i: (q, i, 0)),
            pl.BlockSpec((None, 1, 4, S5_PIECES // 2, FNET_GROUP), lambda q, i: (layer, q, 0, 0, 0)),
            pl.BlockSpec((None, S5_PAIRS, S5_PW, S5_PW), lambda q, i: (layer, q, 0, 0)),
            pl.BlockSpec((None, S5_GQ, S5_PW // 2, S5_PW // 2), lambda q, i: (layer, q, 0, 0)),
            pl.BlockSpec((None, S5_PAIRS, S5_PW, S5_PW), lambda q, i: (layer, q, 0, 0)),
            pl.BlockSpec((None, 1, 1, S5_W), lambda q, i: (layer, q, 0, 0)),
        ],
        out_specs=pl.BlockSpec((1, nc, S5_W), lambda q, i: (q, i, 0)),
        out_shape=jax.ShapeDtypeStruct((S5_Q, b * nc, S5_W), BF16),
        scratch_shapes=[pltpu.VMEM((nc * S5_PIECES, FNET_GROUP), F32)],
        compiler_params=_cparams("parallel", "parallel"),
        name="s5",
    )(c4, decay, mend, toep, rout, dskip)


def _merge_kernel(x_ref, h_ref, fo_ref, sg_ref, y4_ref, g_ref, wga_ref, wgb_ref, wgc_ref,
                  wf_ref, ws_ref, wglu_ref, wo_ref, o_ref, h2_ref, m_ref, y_scr, *, tm):
    h = h_ref[...]
    fo = fo_ref[...].astype(BF16)
    sg = sg_ref[...]
    for q in range(S5_Q):
        for half in range(2):
            by_group = [y4_ref[q, :, (2 * gl + half) * FNET_GROUP:
                               (2 * gl + half + 1) * FNET_GROUP].astype(F32)
                        for gl in range(S5_GQ)]
            for i, blk in enumerate(_granule_transpose(by_group)):
                y_scr[q, pl.ds(S5_GQ * half + i, tm // S5_L, stride=S5_L), :] = blk
    ys = jax.nn.gelu(jnp.concatenate([y_scr[q] for q in range(S5_Q)], axis=1)).astype(BF16)
    nb = 256
    for c0 in range(0, D_MODEL, nb):
        cs = slice(c0, c0 + nb)
        ga = jax.nn.sigmoid(_dot(h, wga_ref[:, cs]))
        gb = jax.nn.sigmoid(_dot(h, wgb_ref[:, cs]))
        gc = jax.nn.sigmoid(_dot(h, wgc_ref[:, cs]))
        ya = _dot(fo, wf_ref[:, cs])
        yb = _dot(sg, ws_ref[:, cs])
        val = _dot(ys, wglu_ref[:, cs])
        gate = _dot(ys, wglu_ref[:, D_MODEL + c0:D_MODEL + c0 + nb])
        yc = val * jax.nn.sigmoid(gate)
        m_ref[:, cs] = (ga * ya + gb * yb + gc * yc).astype(BF16)
    g = g_ref[...]
    for r0 in range(0, tm, SGU_CHUNK):
        rows = slice(r0, r0 + SGU_CHUNK)
        xo = x_ref[rows, :] + _dot(m_ref[rows, :], wo_ref[...])
        o_ref[rows, :] = xo
        h2_ref[rows, :] = _rms(xo, g).astype(BF16)


def _merge(x, h, fo, sg, y4, layer, g_ffn, w_in, wf, ws, wglu, wo):
    gate0 = 4 * BRANCH // D_MODEL
    t = x.shape[0]
    tm = TOKEN_TILE
    return pl.pallas_call(
        functools.partial(_merge_kernel, tm=tm),
        grid=(t // tm,),
        in_specs=[
            pl.BlockSpec((tm, D_MODEL), lambda i: (i, 0)),
            pl.BlockSpec((tm, D_MODEL), lambda i: (i, 0)),
            pl.BlockSpec((tm, BRANCH), lambda i: (i, 0)),
            pl.BlockSpec((tm, BRANCH), lambda i: (i, 0)),
            pl.BlockSpec((S5_Q, tm // S5_L, S5_W), lambda i: (0, i, 0)),
            _layer_spec((1, D_MODEL), layer),
            _layer_spec((D_MODEL, D_MODEL), layer, index=(0, gate0)),
            _layer_spec((D_MODEL, D_MODEL), layer, index=(0, gate0 + 1)),
            _layer_spec((D_MODEL, D_MODEL), layer, index=(0, gate0 + 2)),
            _layer_spec((BRANCH, D_MODEL), layer),
            _layer_spec((BRANCH, D_MODEL), layer),
            _layer_spec((BRANCH, 2 * D_MODEL), layer),
            _layer_spec((D_MODEL, D_MODEL), layer),
        ],
        out_specs=[pl.BlockSpec((tm, D_MODEL), lambda i: (i, 0)),
                   pl.BlockSpec((tm, D_MODEL), lambda i: (i, 0))],
        out_shape=[jax.ShapeDtypeStruct((t, D_MODEL), F32),
                   jax.ShapeDtypeStruct((t, D_MODEL), BF16)],
        scratch_shapes=[pltpu.VMEM((tm, D_MODEL), BF16),
                        pltpu.VMEM((S5_Q, tm, FNET_GROUP), F32)],
        compiler_params=_cparams("parallel"),
        name="merge",
    )(x, h, fo, sg, y4, g_ffn, w_in, w_in, w_in, wf, ws, wglu, wo)


def _ffn_kernel(hp_ref, hm_ref, hn_ref, x_ref, wup_ref, cw_ref, wdn_ref, gf_ref,
                o_ref, h_ref, up_ref, act_ref, *, tm, seq, final):
    pos = lax.rem(pl.program_id(0) * tm, seq)
    zero = jnp.zeros((HALO, D_MODEL), BF16)
    h_ref[0:HALO] = jnp.where(pos != 0, hp_ref[...], zero)
    h_ref[HALO:HALO + tm] = hm_ref[...]
    h_ref[HALO + tm:] = jnp.where(pos + tm != seq, hn_ref[...], zero)
    hext = h_ref[...]
    rows = tm + 2 * HALO
    for j in range(FF_NBLK):
        gate_cols = slice(j * FF_BLK, (j + 1) * FF_BLK)
        val_cols = slice(D_FF + j * FF_BLK, D_FF + (j + 1) * FF_BLK)
        up = up_ref.at[j % 2]
        up[:, :FF_BLK] = _dot(hext, wup_ref[:, gate_cols])
        up[:, FF_BLK:] = _dot(hext, wup_ref[:, val_cols])
        cw = jnp.concatenate([cw_ref[:, gate_cols], cw_ref[:, val_cols]], axis=1)
        u = up[...]
        conv = (pltpu.roll(u, 1, 0) * cw[0:1] + u * cw[1:2]
                + pltpu.roll(u, rows - 1, 0) * cw[2:3] + cw[3:4])[HALO:HALO + tm]
        gate = conv[:, :FF_BLK]
        half = 0.5 * gate
        act = (half + half * jnp.tanh(half)) * conv[:, FF_BLK:]
        act_ref[:, j * FF_BLK:(j + 1) * FF_BLK] = act.astype(BF16)
    y = x_ref[...] + _dot(act_ref[...], wdn_ref[...])
    if final:
        y = _rms(y, gf_ref[...])
    o_ref[...] = y


def _ffn(x, h, layer, wup, cw, wdn, gf, seq, final):
    t = x.shape[0]
    tm = FFN_TILE
    per = tm // HALO
    last = t // HALO - 1
    rows = tm + 2 * HALO
    return pl.pallas_call(
        functools.partial(_ffn_kernel, tm=tm, seq=seq, final=final),
        grid=(t // tm,),
        in_specs=[
            pl.BlockSpec((HALO, D_MODEL), lambda i: (jnp.maximum(i * per - 1, 0), 0)),
            pl.BlockSpec((tm, D_MODEL), lambda i: (i, 0)),
            pl.BlockSpec((HALO, D_MODEL), lambda i: (jnp.minimum((i + 1) * per, last), 0)),
            pl.BlockSpec((tm, D_MODEL), lambda i: (i, 0)),
            _layer_spec((D_MODEL, 2 * D_FF), layer, resident=True),
            _layer_spec((8, 2 * D_FF), layer),
            _layer_spec((D_FF, D_MODEL), layer, resident=True),
            _const_spec((1, D_MODEL)),
        ],
        out_specs=pl.BlockSpec((tm, D_MODEL), lambda i: (i, 0)),
        out_shape=jax.ShapeDtypeStruct((t, D_MODEL), F32),
        scratch_shapes=[pltpu.VMEM((rows, D_MODEL), BF16),
                        pltpu.VMEM((2, rows, 2 * FF_BLK), F32),
                        pltpu.VMEM((tm, D_FF), BF16)],
        compiler_params=_cparams("parallel"),
        name="ffn_final" if final else "ffn",
    )(h, h, h, x, wup, cw, wdn, gf)


def _dft_consts(n1):
    n2 = FFT_N2
    s = n1 * n2
    k = np.arange(FNET_GROUP)
    ang = 2 * np.pi * ((k[:, None] * k[None, :]) % FNET_GROUP) / FNET_GROUP
    fc = np.concatenate([np.cos(ang), -np.sin(ang)], axis=1) / math.sqrt(FNET_GROUP)
    k1 = np.arange(n1)[None, :, None]
    s1 = np.arange(n1)[None, None, :]
    s2 = np.arange(n2)[:, None, None]
    ang = 2 * np.pi * ((k1 * (n2 * s1 + s2)) % s) / s
    c, sn = np.cos(ang), np.sin(ang)
    f1 = np.concatenate([np.concatenate([c, sn], axis=2),
                         np.concatenate([-sn, c], axis=2)], axis=1) / math.sqrt(n1)
    k2 = np.arange(n2)
    ang = 2 * np.pi * ((k2[:, None] * k2[None, :]) % n2) / n2
    f2 = np.concatenate([np.cos(ang), np.sin(ang)], axis=1) / math.sqrt(n2)
    return (jnp.asarray(fc, BF16), jnp.asarray(f1, BF16), jnp.asarray(f2, BF16))


def _cmul(ar, ai, br, bi):
    return ar * br - ai * bi, ar * bi + ai * br


def _s5_operators(lam_re, lam_im, log_dt, b_re, b_im, c_re, c_im, d_skip):
    L, G, P, H, Q, GQ = S5_L, S5_G, S5_P, S5_H, S5_Q, S5_GQ
    dt = jnp.exp(log_dt)[..., None]
    lr, li = lam_re * dt, lam_im * dt

    def lb_pow(n):
        n = jnp.asarray(n, F32)[:, None, None, None]
        mag = jnp.exp(n * lr)
        return mag * jnp.cos(n * li), mag * jnp.sin(n * li)

    lbr, lbi = lb_pow([1.0])
    den = lam_re * lam_re + lam_im * lam_im
    bsr, bsi = _cmul(lbr[0] - 1.0, lbi[0], lam_re / den, -lam_im / den)
    cbr, cbi = _cmul(c_re, c_im, bsr[:, :, None, :], bsi[:, :, None, :])

    pw_r, pw_i = lb_pow(np.arange(L + 1))

    def lag_kernel(d):
        wr, wi = _cmul(cbr[d][None], cbi[d][None],
                       pw_r[:L, d][:, :, None, :], pw_i[:L, d][:, :, None, :])
        return (jnp.einsum('sghp,gpk->sghk', wr, b_re)
                - jnp.einsum('sghp,gpk->sghk', wi, b_im))

    kf, kb = lag_kernel(0), lag_kernel(1)
    klag = jnp.concatenate([kb[1:][::-1], (kf[0] + kb[0])[None], kf[1:]], axis=0)
    idx = np.arange(L)[None, :] - np.arange(L)[:, None] + (L - 1)
    kss = klag[idx]
    gw = L * H
    toep = jnp.transpose(kss, (2, 0, 4, 1, 3)).reshape(G, gw, gw)
    pair_eye = jnp.eye(2, dtype=F32)

    def end_state(d, exps):
        er, ei = pw_r[exps, d], pw_i[exps, d]
        return _cmul(er[..., None], ei[..., None], b_re[None], b_im[None])

    fr, fi = end_state(0, np.arange(L)[::-1])
    br_, bi_ = end_state(1, np.arange(L))
    parts = jnp.stack([fr, fi, br_, bi_], axis=0)
    parts = jnp.transpose(parts, (2, 1, 4, 0, 3)).reshape(G // 2, 2, gw, 4, P)
    mend = jnp.einsum('ngrcp,gk->ngrckp', parts, pair_eye).reshape(G // 2, 2 * gw, 4 * 2 * P)

    def read_out(d, exps):
        wr, wi = _cmul(cbr[d][None], cbi[d][None],
                       pw_r[exps, d][:, :, None, :], pw_i[exps, d][:, :, None, :])
        return wr, -wi

    rfr, rfi = read_out(0, np.arange(1, L + 1))
    rbr, rbi = read_out(1, np.arange(L, 0, -1))
    parts = jnp.stack([rfr, rfi, rbr, rbi], axis=0)
    parts = jnp.transpose(parts, (2, 0, 4, 1, 3)).reshape(G // 2, 2, 4, P, gw)
    rout = jnp.einsum('ngcpr,gk->ncgpkr', parts, pair_eye).reshape(G // 2, 4 * 2 * P, 2 * gw)

    ar = pw_r[L].reshape(2, Q, GQ * P // FNET_GROUP, FNET_GROUP)
    ai = pw_i[L].reshape(2, Q, GQ * P // FNET_GROUP, FNET_GROUP)
    a_re = jnp.concatenate([ar, ar], axis=2)
    a_im = jnp.concatenate([-ai, ai], axis=2)
    decay = jnp.stack([a_re[0], a_im[0], a_re[1], a_im[1]], axis=1)

    dtile = jnp.broadcast_to(d_skip[:, None, :], (G, L, H)).reshape(Q, 1, S5_W)
    return mend.astype(BF16), decay, toep.astype(BF16), rout.astype(BF16), dtile


def _pack_sgu(sgu_w, sgu_b):
    ws = jnp.concatenate([sgu_w[0::2], sgu_w[1::2]], axis=2).astype(BF16)
    sb = jnp.repeat(sgu_b.T, SGU_HEAD_DIM, axis=1)
    return ws, sb


def _trunk(x, p, consts):
    b, seq, _ = x.shape
    n1 = seq // FFT_N2
    nc = seq // S5_L
    t = b * seq
    fc, f1, f2 = consts
    depth = p["w_in"].shape[0]
    x = x.reshape(t, D_MODEL)
    for li in range(depth):
        h, xf, sg, c4 = _proj(x, li, p["g_mix"], p["w_in"], fc, p["v_gain"], p["sgu_w"],
                              p["sgu_b"], b, n1)
        fo = _fft2(_fft1(xf, f1, b, n1), f2, b, n1).reshape(t, BRANCH)
        y4 = _s5(c4, li, p["decay"], p["mend"], p["toep"], p["rout"], p["dtile"], b, nc)
        x, h2 = _merge(x, h, fo, sg, y4, li, p["g_ffn"], p["w_in"], p["w_fnet"], p["w_sgu"],
                       p["w_glu"], p["w_out"])
        x = _ffn(x, h2, li, p["w_up"], p["conv"], p["w_down"], p["g_final"], seq,
                 final=(li == depth - 1))
    return x.reshape(b, seq, D_MODEL)


def kernel(x_prompt, x_sample, g_mix, w_in, w_fnet, sgu_v_gain, sgu_w, sgu_b, w_sgu,
           s5_lam_re, s5_lam_im, s5_log_dt, s5_b_re, s5_b_im, s5_c_re, s5_c_im, s5_d,
           w_glu, w_out, g_ffn, w_up, conv_w, conv_b, w_down, g_final):
    depth = w_in.shape[0]
    ws, sb = jax.vmap(_pack_sgu)(sgu_w, sgu_b)
    mend, decay, toep, rout, dtile = jax.vmap(_s5_operators)(
        s5_lam_re, s5_lam_im, s5_log_dt, s5_b_re, s5_b_im, s5_c_re, s5_c_im, s5_d)
    conv = jnp.concatenate([conv_w, conv_b[:, None], jnp.zeros((depth, 4, 2 * D_FF), F32)], axis=1)
    params = dict(
        g_mix=g_mix[:, None], w_in=w_in.astype(BF16), v_gain=sgu_v_gain[:, None],
        sgu_w=ws, sgu_b=sb, w_fnet=w_fnet.astype(BF16), w_sgu=w_sgu.astype(BF16),
        w_glu=w_glu.astype(BF16), w_out=w_out.astype(BF16),
        mend=mend, decay=decay, toep=toep, rout=rout, dtile=dtile,
        g_ffn=g_ffn[:, None], w_up=w_up.astype(BF16), conv=conv, w_down=w_down.astype(BF16),
        g_final=g_final[None])
    outs = []
    for x in (x_prompt, x_sample):
        outs.append(_trunk(x, params, _dft_consts(x.shape[1] // FFT_N2)))
    return tuple(outs)
```

```python
import functools
import math

import numpy as np
import jax
import jax.numpy as jnp
from jax import lax
from jax.experimental import pallas as pl
from jax.experimental.pallas import tpu as pltpu

F32 = jnp.float32
BF16 = jnp.bfloat16

D_MODEL = 1024
BRANCH = 512
FNET_GROUP = 128
FNET_GROUPS = BRANCH // FNET_GROUP
SGU_CHUNK = 128
SGU_HEADS = 8
SGU_HEAD_DIM = BRANCH // SGU_HEADS
S5_H = 16
S5_G = BRANCH // S5_H
S5_P = 64
S5_L = 16
S5_Q = 4
S5_GQ = S5_G // S5_Q
S5_W = S5_L * FNET_GROUP
S5_PIECES = S5_W // FNET_GROUP
S5_PAIRS = S5_GQ // 2
S5_PW = 2 * S5_L * S5_H
D_FF = 2816
FF_BLK = 256
FF_NBLK = D_FF // FF_BLK
RMS_EPS = 1e-6
FFT_N2 = 128
HALO = 16

TOKEN_TILE = 512
PROJ_TILE = 1024
FFN_TILE = 1024
VMEM_LIMIT = 60 * 1024 * 1024


def _cparams(*sem):
    return pltpu.CompilerParams(dimension_semantics=sem, vmem_limit_bytes=VMEM_LIMIT)


def _const_spec(shape):
    nd = len(shape)
    return pl.BlockSpec(shape, lambda *_: (0,) * nd)


def _layer_spec(shape, layer, index=None, resident=False):
    idx = (layer,) + ((0,) * len(shape) if index is None else tuple(index))
    kwargs = dict(pipeline_mode=pl.Buffered(1)) if resident else {}
    return pl.BlockSpec((None,) + tuple(shape), lambda *_: idx, **kwargs)


def _dot(a, b):
    return jnp.dot(a, b, preferred_element_type=F32)


def _rms(x, g):
    return x * lax.rsqrt(jnp.mean(x * x, axis=-1, keepdims=True) + RMS_EPS) * g


def _pack_complex(re, im):
    return pltpu.pack_elementwise([re, im], packed_dtype=BF16)


def _unpack_complex(w):
    re = pltpu.unpack_elementwise(w, index=0, packed_dtype=BF16, unpacked_dtype=F32)
    im = pltpu.unpack_elementwise(w, index=1, packed_dtype=BF16, unpacked_dtype=F32)
    return jnp.concatenate([re, im], axis=0).astype(BF16)


def _granule_transpose(arrs):
    lane = lax.broadcasted_iota(jnp.int32, arrs[0].shape, 1)
    a = list(arrs)
    for d in (4, 2, 1):
        upper = ((lane // S5_H) & d) != 0
        for i in range(S5_GQ):
            if i & d:
                continue
            lo, hi = a[i], a[i + d]
            a[i] = jnp.where(upper, pltpu.roll(hi, d * S5_H, 1), lo)
            a[i + d] = jnp.where(upper, hi, pltpu.roll(lo, FNET_GROUP - d * S5_H, 1))
    return a


def _proj_kernel(x_ref, g_ref, w_ref, fc_ref, vg_ref, ws_ref, sb_ref,
                 h_ref, xf_ref, sg_ref, c4_ref, c_scr, *, tm):
    h = _rms(x_ref[...], g_ref[...]).astype(BF16)
    h_ref[...] = h

    v = _dot(h, w_ref[:, 2 * BRANCH:3 * BRANCH])
    u = _dot(h, w_ref[:, BRANCH:2 * BRANCH])
    c = _dot(h, w_ref[:, 3 * BRANCH:4 * BRANCH])
    a = _dot(h, w_ref[:, 0:BRANCH]).astype(BF16)

    v = (_rms(jax.nn.gelu(v), vg_ref[...])).astype(BF16)
    u = jax.nn.gelu(u)
    lane = lax.broadcasted_iota(jnp.int32, (SGU_CHUNK, FNET_GROUP), 1)
    first_head = lane < SGU_HEAD_DIM
    zero = jnp.zeros((SGU_CHUNK, FNET_GROUP), BF16)
    for ck in range(tm // SGU_CHUNK):
        rows = slice(ck * SGU_CHUNK, (ck + 1) * SGU_CHUNK)
        for j in range(SGU_HEADS // 2):
            cols = slice(j * FNET_GROUP, (j + 1) * FNET_GROUP)
            blk = v[rows, cols]
            rhs = jnp.concatenate([jnp.where(first_head, blk, zero),
                                   jnp.where(first_head, zero, blk)], axis=0)
            mixed = _dot(ws_ref[j], rhs) + sb_ref[:, cols]
            sg_ref[rows, cols] = (u[rows, cols] * mixed).astype(BF16)

    for q in range(S5_Q):
        c_scr[q] = c[:, q * FNET_GROUP:(q + 1) * FNET_GROUP]
    for q in range(S5_Q):
        for half in range(2):
            by_time = [c_scr[q, pl.ds(S5_GQ * half + i, tm // S5_L, stride=S5_L), :]
                       for i in range(S5_GQ)]
            for gl, blk in enumerate(_granule_transpose(by_time)):
                lo = gl * 2 * FNET_GROUP + half * FNET_GROUP
                c4_ref[q, :, lo:lo + FNET_GROUP] = blk.astype(BF16)

    for g in range(FNET_GROUPS):
        lo, hi = g * FNET_GROUP, (g + 1) * FNET_GROUP
        xc = _dot(a[:, lo:hi], fc_ref[...])
        packed = _pack_complex(xc[:, :FNET_GROUP], xc[:, FNET_GROUP:])
        for j in range(tm // FFT_N2):
            xf_ref[0, :, j, lo:hi] = packed[j * FFT_N2:(j + 1) * FFT_N2]


def _proj(x, layer, g, w, fc, vg, ws, sb, b, n1):
    t = x.shape[0]
    tm = PROJ_TILE
    s1_blk = tm // FFT_N2
    per_seq = n1 // s1_blk
    return pl.pallas_call(
        functools.partial(_proj_kernel, tm=tm),
        grid=(t // tm,),
        in_specs=[
            pl.BlockSpec((tm, D_MODEL), lambda i: (i, 0)),
            _layer_spec((1, D_MODEL), layer),
            _layer_spec((D_MODEL, 4 * BRANCH), layer, resident=True),
            _const_spec((FNET_GROUP, 2 * FNET_GROUP)),
            _layer_spec((1, BRANCH), layer),
            _layer_spec((SGU_HEADS // 2, SGU_CHUNK, 2 * SGU_CHUNK), layer),
            _layer_spec((SGU_CHUNK, BRANCH), layer),
        ],
        out_specs=[
            pl.BlockSpec((tm, D_MODEL), lambda i: (i, 0)),
            pl.BlockSpec((1, FFT_N2, s1_blk, BRANCH), lambda i: (i // per_seq, 0, i % per_seq, 0)),
            pl.BlockSpec((tm, BRANCH), lambda i: (i, 0)),
            pl.BlockSpec((S5_Q, tm // S5_L, S5_W), lambda i: (0, i, 0)),
        ],
        out_shape=[
            jax.ShapeDtypeStruct((t, D_MODEL), BF16),
            jax.ShapeDtypeStruct((b, FFT_N2, n1, BRANCH), jnp.uint32),
            jax.ShapeDtypeStruct((t, BRANCH), BF16),
            jax.ShapeDtypeStruct((S5_Q, t // S5_L, S5_W), BF16),
        ],
        scratch_shapes=[pltpu.VMEM((S5_Q, tm, FNET_GROUP), F32)],
        compiler_params=_cparams("parallel"),
        name="proj",
    )(x, g, w, fc, vg, ws, sb)


def _fft1_kernel(x_ref, f_ref, z_ref, *, n1, sb):
    for j in range(sb):
        z = _dot(f_ref[j], _unpack_complex(x_ref[0, j]))
        z_ref[0, :, j, :] = _pack_complex(z[:n1], z[n1:])


def _fft1(xf, f1, b, n1):
    sb = 32 if n1 <= 64 else 16
    return pl.pallas_call(
        functools.partial(_fft1_kernel, n1=n1, sb=sb),
        grid=(b, FFT_N2 // sb),
        in_specs=[
            pl.BlockSpec((1, sb, n1, BRANCH), lambda i, j: (i, j, 0, 0)),
            pl.BlockSpec((sb, 2 * n1, 2 * n1), lambda i, j: (j, 0, 0)),
        ],
        out_specs=pl.BlockSpec((1, n1, sb, BRANCH), lambda i, j: (i, 0, j, 0)),
        out_shape=jax.ShapeDtypeStruct((b, n1, FFT_N2, BRANCH), jnp.uint32),
        compiler_params=_cparams("parallel", "parallel"),
        name="fft1",
    )(xf, f1)


def _fft2_kernel(z_ref, f_ref, o_ref, *, kb):
    for k in range(kb):
        o_ref[0, :, k, :] = _dot(f_ref[...], _unpack_complex(z_ref[0, k]))


def _fft2(z, f2, b, n1):
    kb = 16 if n1 >= 16 else 8
    return pl.pallas_call(
        functools.partial(_fft2_kernel, kb=kb),
        grid=(b, n1 // kb),
        in_specs=[
            pl.BlockSpec((1, kb, FFT_N2, BRANCH), lambda i, j: (i, j, 0, 0)),
            _const_spec((FFT_N2, 2 * FFT_N2)),
        ],
        out_specs=pl.BlockSpec((1, FFT_N2, kb, BRANCH), lambda i, j: (i, 0, j, 0)),
        out_shape=jax.ShapeDtypeStruct((b, FFT_N2, n1, BRANCH), F32),
        compiler_params=_cparams("parallel", "parallel"),
        name="fft2",
    )(z, f2)


def _s5_kernel(x_ref, a_ref, m_ref, t_ref, r_ref, d_ref, y_ref, st_ref, *, nc):
    half = S5_PIECES // 2
    gw = S5_PW // 2

    for pp in range(S5_PAIRS):
        e = _dot(x_ref[0, :, pp * S5_PW:(pp + 1) * S5_PW], m_ref[pp])
        for k in range(4):
            st_ref[pl.ds(S5_PAIRS * k + pp, nc, stride=S5_PIECES), :] = (
                e[:, k * FNET_GROUP:(k + 1) * FNET_GROUP])

    arf, aif, arb, aib = a_ref[0, 0], a_ref[0, 1], a_ref[0, 2], a_ref[0, 3]

    def body(n, carry):
        hf, hb = carry
        fwd = pl.ds(pl.multiple_of(n * S5_PIECES, S5_PIECES), half)
        bwd = pl.ds(pl.multiple_of((nc - 1 - n) * S5_PIECES + half, half), half)
        ef = st_ref[fwd, :]
        eb = st_ref[bwd, :]
        st_ref[fwd, :] = hf
        st_ref[bwd, :] = hb
        hf = arf * hf + aif * pltpu.roll(hf, half // 2, 0) + ef
        hb = arb * hb + aib * pltpu.roll(hb, half // 2, 0) + eb
        return hf, hb

    zero = jnp.zeros((half, FNET_GROUP), F32)
    lax.fori_loop(0, nc, body, (zero, zero), unroll=4)

    for pp in range(S5_PAIRS):
        cols = slice(pp * S5_PW, (pp + 1) * S5_PW)
        x = x_ref[0, :, cols]
        h = jnp.concatenate([st_ref[pl.ds(S5_PAIRS * k + pp, nc, stride=S5_PIECES), :]
                             for k in range(4)], axis=1)
        y = jnp.concatenate([_dot(x[:, :gw], t_ref[2 * pp]),
                             _dot(x[:, gw:], t_ref[2 * pp + 1])], axis=1)
        y = y + _dot(h.astype(BF16), r_ref[pp]) + x.astype(F32) * d_ref[0, :, cols]
        y_ref[0, :, cols] = y.astype(BF16)


def _s5(c4, layer, decay, mend, toep, rout, dskip, b, nc):
    return pl.pallas_call(
        functools.partial(_s5_kernel, nc=nc),
        grid=(S5_Q, b),
        in_specs=[
            pl.BlockSpec((1, nc, S5_W), lambda q, i: (q, i, 0)),
            pl.BlockSpec((None, 1, 4, S5_PIECES // 2, FNET_GROUP), lambda q, i: (layer, q, 0, 0, 0)),
            pl.BlockSpec((None, S5_PAIRS, S5_PW, S5_PW), lambda q, i: (layer, q, 0, 0)),
            pl.BlockSpec((None, S5_GQ, S5_PW // 2, S5_PW // 2), lambda q, i: (layer, q, 0, 0)),
            pl.BlockSpec((None, S5_PAIRS, S5_PW, S5_PW), lambda q, i: (layer, q, 0, 0)),
            pl.BlockSpec((None, 1, 1, S5_W), lambda q, i: (layer, q, 0, 0)),
        ],
        out_specs=pl.BlockSpec((1, nc, S5_W), lambda q, i: (q, i, 0)),
        out_shape=jax.ShapeDtypeStruct((S5_Q, b * nc, S5_W), BF16),
        scratch_shapes=[pltpu.VMEM((nc * S5_PIECES, FNET_GROUP), F32)],
        compiler_params=_cparams("parallel", "parallel"),
        name="s5",
    )(c4, decay, mend, toep, rout, dskip)


def _merge_kernel(x_ref, h_ref, fo_ref, sg_ref, y4_ref, g_ref, wga_ref, wgb_ref, wgc_ref,
                  wf_ref, ws_ref, wglu_ref, wo_ref, o_ref, h2_ref, m_ref, y_scr, *, tm):
    h = h_ref[...]
    fo = fo_ref[...].astype(BF16)
    sg = sg_ref[...]
    for q in range(S5_Q):
        for half in range(2):
            by_group = [y4_ref[q, :, (2 * gl + half) * FNET_GROUP:
                               (2 * gl + half + 1) * FNET_GROUP].astype(F32)
                        for gl in range(S5_GQ)]
            for i, blk in enumerate(_granule_transpose(by_group)):
                y_scr[q, pl.ds(S5_GQ * half + i, tm // S5_L, stride=S5_L), :] = blk
    ys = jax.nn.gelu(jnp.concatenate([y_scr[q] for q in range(S5_Q)], axis=1)).astype(BF16)
    nb = 256
    for c0 in range(0, D_MODEL, nb):
        cs = slice(c0, c0 + nb)
        ga = jax.nn.sigmoid(_dot(h, wga_ref[:, cs]))
        gb = jax.nn.sigmoid(_dot(h, wgb_ref[:, cs]))
        gc = jax.nn.sigmoid(_dot(h, wgc_ref[:, cs]))
        ya = _dot(fo, wf_ref[:, cs])
        yb = _dot(sg, ws_ref[:, cs])
        val = _dot(ys, wglu_ref[:, cs])
        gate = _dot(ys, wglu_ref[:, D_MODEL + c0:D_MODEL + c0 + nb])
        yc = val * jax.nn.sigmoid(gate)
        m_ref[:, cs] = (ga * ya + gb * yb + gc * yc).astype(BF16)
    g = g_ref[...]
    for r0 in range(0, tm, SGU_CHUNK):
        rows = slice(r0, r0 + SGU_CHUNK)
        xo = x_ref[rows, :] + _dot(m_ref[rows, :], wo_ref[...])
        o_ref[rows, :] = xo
        h2_ref[rows, :] = _rms(xo, g).astype(BF16)


def _merge(x, h, fo, sg, y4, layer, g_ffn, w_in, wf, ws, wglu, wo):
    gate0 = 4 * BRANCH // D_MODEL
    t = x.shape[0]
    tm = TOKEN_TILE
    return pl.pallas_call(
        functools.partial(_merge_kernel, tm=tm),
        grid=(t // tm,),
        in_specs=[
            pl.BlockSpec((tm, D_MODEL), lambda i: (i, 0)),
            pl.BlockSpec((tm, D_MODEL), lambda i: (i, 0)),
            pl.BlockSpec((tm, BRANCH), lambda i: (i, 0)),
            pl.BlockSpec((tm, BRANCH), lambda i: (i, 0)),
            pl.BlockSpec((S5_Q, tm // S5_L, S5_W), lambda i: (0, i, 0)),
            _layer_spec((1, D_MODEL), layer),
            _layer_spec((D_MODEL, D_MODEL), layer, index=(0, gate0)),
            _layer_spec((D_MODEL, D_MODEL), layer, index=(0, gate0 + 1)),
            _layer_spec((D_MODEL, D_MODEL), layer, index=(0, gate0 + 2)),
            _layer_spec((BRANCH, D_MODEL), layer),
            _layer_spec((BRANCH, D_MODEL), layer),
            _layer_spec((BRANCH, 2 * D_MODEL), layer),
            _layer_spec((D_MODEL, D_MODEL), layer),
        ],
        out_specs=[pl.BlockSpec((tm, D_MODEL), lambda i: (i, 0)),
                   pl.BlockSpec((tm, D_MODEL), lambda i: (i, 0))],
        out_shape=[jax.ShapeDtypeStruct((t, D_MODEL), F32),
                   jax.ShapeDtypeStruct((t, D_MODEL), BF16)],
        scratch_shapes=[pltpu.VMEM((tm, D_MODEL), BF16),
                        pltpu.VMEM((S5_Q, tm, FNET_GROUP), F32)],
        compiler_params=_cparams("parallel"),
        name="merge",
    )(x, h, fo, sg, y4, g_ffn, w_in, w_in, w_in, wf, ws, wglu, wo)


def _ffn_kernel(hp_ref, hm_ref, hn_ref, x_ref, wup_ref, cw_ref, wdn_ref, gf_ref,
                o_ref, h_ref, up_ref, act_ref, *, tm, seq, final):
    pos = lax.rem(pl.program_id(0) * tm, seq)
    zero = jnp.zeros((HALO, D_MODEL), BF16)
    h_ref[0:HALO] = jnp.where(pos != 0, hp_ref[...], zero)
    h_ref[HALO:HALO + tm] = hm_ref[...]
    h_ref[HALO + tm:] = jnp.where(pos + tm != seq, hn_ref[...], zero)
    hext = h_ref[...]
    rows = tm + 2 * HALO
    for j in range(FF_NBLK):
        gate_cols = slice(j * FF_BLK, (j + 1) * FF_BLK)
        val_cols = slice(D_FF + j * FF_BLK, D_FF + (j + 1) * FF_BLK)
        up = up_ref.at[j % 2]
        up[:, :FF_BLK] = _dot(hext, wup_ref[:, gate_cols])
        up[:, FF_BLK:] = _dot(hext, wup_ref[:, val_cols])
        cw = jnp.concatenate([cw_ref[:, gate_cols], cw_ref[:, val_cols]], axis=1)
        u = up[...]
        conv = (pltpu.roll(u, 1, 0) * cw[0:1] + u * cw[1:2]
                + pltpu.roll(u, rows - 1, 0) * cw[2:3] + cw[3:4])[HALO:HALO + tm]
        gate = conv[:, :FF_BLK]
        half = 0.5 * gate
        act = (half + half * jnp.tanh(half)) * conv[:, FF_BLK:]
        act_ref[:, j * FF_BLK:(j + 1) * FF_BLK] = act.astype(BF16)
    y = x_ref[...] + _dot(act_ref[...], wdn_ref[...])
    if final:
        y = _rms(y, gf_ref[...])
    o_ref[...] = y


def _ffn(x, h, layer, wup, cw, wdn, gf, seq, final):
    t = x.shape[0]
    tm = FFN_TILE
    per = tm // HALO
    last = t // HALO - 1
    rows = tm + 2 * HALO
    return pl.pallas_call(
        functools.partial(_ffn_kernel, tm=tm, seq=seq, final=final),
        grid=(t // tm,),
        in_specs=[
            pl.BlockSpec((HALO, D_MODEL), lambda i: (jnp.maximum(i * per - 1, 0), 0)),
            pl.BlockSpec((tm, D_MODEL), lambda i: (i, 0)),
            pl.BlockSpec((HALO, D_MODEL), lambda i: (jnp.minimum((i + 1) * per, last), 0)),
            pl.BlockSpec((tm, D_MODEL), lambda i: (i, 0)),
            _layer_spec((D_MODEL, 2 * D_FF), layer, resident=True),
            _layer_spec((8, 2 * D_FF), layer),
            _layer_spec((D_FF, D_MODEL), layer, resident=True),
            _const_spec((1, D_MODEL)),
        ],
        out_specs=pl.BlockSpec((tm, D_MODEL), lambda i: (i, 0)),
        out_shape=jax.ShapeDtypeStruct((t, D_MODEL), F32),
        scratch_shapes=[pltpu.VMEM((rows, D_MODEL), BF16),
                        pltpu.VMEM((2, rows, 2 * FF_BLK), F32),
                        pltpu.VMEM((tm, D_FF), BF16)],
        compiler_params=_cparams("parallel"),
        name="ffn_final" if final else "ffn",
    )(h, h, h, x, wup, cw, wdn, gf)


def _dft_consts(n1):
    n2 = FFT_N2
    s = n1 * n2
    k = np.arange(FNET_GROUP)
    ang = 2 * np.pi * ((k[:, None] * k[None, :]) % FNET_GROUP) / FNET_GROUP
    fc = np.concatenate([np.cos(ang), -np.sin(ang)], axis=1) / math.sqrt(FNET_GROUP)
    k1 = np.arange(n1)[None, :, None]
    s1 = np.arange(n1)[None, None, :]
    s2 = np.arange(n2)[:, None, None]
    ang = 2 * np.pi * ((k1 * (n2 * s1 + s2)) % s) / s
    c, sn = np.cos(ang), np.sin(ang)
    f1 = np.concatenate([np.concatenate([c, sn], axis=2),
                         np.concatenate([-sn, c], axis=2)], axis=1) / math.sqrt(n1)
    k2 = np.arange(n2)
    ang = 2 * np.pi * ((k2[:, None] * k2[None, :]) % n2) / n2
    f2 = np.concatenate([np.cos(ang), np.sin(ang)], axis=1) / math.sqrt(n2)
    return (jnp.asarray(fc, BF16), jnp.asarray(f1, BF16), jnp.asarray(f2, BF16))


def _cmul(ar, ai, br, bi):
    return ar * br - ai * bi, ar * bi + ai * br


def _s5_operators(lam_re, lam_im, log_dt, b_re, b_im, c_re, c_im, d_skip):
    L, G, P, H, Q, GQ = S5_L, S5_G, S5_P, S5_H, S5_Q, S5_GQ
    dt = jnp.exp(log_dt)[..., None]
    lr, li = lam_re * dt, lam_im * dt

    def lb_pow(n):
        n = jnp.asarray(n, F32)[:, None, None, None]
        mag = jnp.exp(n * lr)
        return mag * jnp.cos(n * li), mag * jnp.sin(n * li)

    lbr, lbi = lb_pow([1.0])
    den = lam_re * lam_re + lam_im * lam_im
    bsr, bsi = _cmul(lbr[0] - 1.0, lbi[0], lam_re / den, -lam_im / den)
    cbr, cbi = _cmul(c_re, c_im, bsr[:, :, None, :], bsi[:, :, None, :])

    pw_r, pw_i = lb_pow(np.arange(L + 1))

    def lag_kernel(d):
        wr, wi = _cmul(cbr[d][None], cbi[d][None],
                       pw_r[:L, d][:, :, None, :], pw_i[:L, d][:, :, None, :])
        return (jnp.einsum('sghp,gpk->sghk', wr, b_re)
                - jnp.einsum('sghp,gpk->sghk', wi, b_im))

    kf, kb = lag_kernel(0), lag_kernel(1)
    klag = jnp.concatenate([kb[1:][::-1], (kf[0] + kb[0])[None], kf[1:]], axis=0)
    idx = np.arange(L)[None, :] - np.arange(L)[:, None] + (L - 1)
    kss = klag[idx].astype(BF16)
    gw = L * H
    toep = jnp.transpose(kss, (2, 0, 4, 1, 3)).reshape(G, gw, gw)
    pair_eye = jnp.eye(2, dtype=BF16)

    def end_state(d, exps):
        er, ei = pw_r[exps, d], pw_i[exps, d]
        return _cmul(er[..., None], ei[..., None], b_re[None], b_im[None])

    fr, fi = end_state(0, np.arange(L)[::-1])
    br_, bi_ = end_state(1, np.arange(L))
    parts = jnp.stack([fr, fi, br_, bi_], axis=0).astype(BF16)
    parts = jnp.transpose(parts, (2, 1, 4, 0, 3)).reshape(G // 2, 2, gw, 4, P)
    mend = jnp.einsum('ngrcp,gk->ngrckp', parts, pair_eye).reshape(G // 2, 2 * gw, 4 * 2 * P)

    def read_out(d, exps):
        wr, wi = _cmul(cbr[d][None], cbi[d][None],
                       pw_r[exps, d][:, :, None, :], pw_i[exps, d][:, :, None, :])
        return wr, -wi

    rfr, rfi = read_out(0, np.arange(1, L + 1))
    rbr, rbi = read_out(1, np.arange(L, 0, -1))
    parts = jnp.stack([rfr, rfi, rbr, rbi], axis=0).astype(BF16)
    parts = jnp.transpose(parts, (2, 0, 4, 1, 3)).reshape(G // 2, 2, 4, P, gw)
    rout = jnp.einsum('ngcpr,gk->ncgpkr', parts, pair_eye).reshape(G // 2, 4 * 2 * P, 2 * gw)

    ar = pw_r[L].reshape(2, Q, GQ * P // FNET_GROUP, FNET_GROUP)
    ai = pw_i[L].reshape(2, Q, GQ * P // FNET_GROUP, FNET_GROUP)
    a_re = jnp.concatenate([ar, ar], axis=2)
    a_im = jnp.concatenate([-ai, ai], axis=2)
    decay = jnp.stack([a_re[0], a_im[0], a_re[1], a_im[1]], axis=1)

    dtile = jnp.broadcast_to(d_skip[:, None, :], (G, L, H)).reshape(Q, 1, S5_W)
    return mend.astype(BF16), decay, toep.astype(BF16), rout.astype(BF16), dtile


def _pack_sgu(sgu_w, sgu_b):
    ws = jnp.concatenate([sgu_w[0::2], sgu_w[1::2]], axis=2).astype(BF16)
    sb = jnp.repeat(sgu_b.T, SGU_HEAD_DIM, axis=1)
    return ws, sb


def _trunk(x, p, consts):
    b, seq, _ = x.shape
    n1 = seq // FFT_N2
    nc = seq // S5_L
    t = b * seq
    fc, f1, f2 = consts
    depth = p["w_in"].shape[0]
    x = x.reshape(t, D_MODEL)
    for li in range(depth):
        h, xf, sg, c4 = _proj(x, li, p["g_mix"], p["w_in"], fc, p["v_gain"], p["sgu_w"],
                              p["sgu_b"], b, n1)
        fo = _fft2(_fft1(xf, f1, b, n1), f2, b, n1).reshape(t, BRANCH)
        y4 = _s5(c4, li, p["decay"], p["mend"], p["toep"], p["rout"], p["dtile"], b, nc)
        x, h2 = _merge(x, h, fo, sg, y4, li, p["g_ffn"], p["w_in"], p["w_fnet"], p["w_sgu"],
                       p["w_glu"], p["w_out"])
        x = _ffn(x, h2, li, p["w_up"], p["conv"], p["w_down"], p["g_final"], seq,
                 final=(li == depth - 1))
    return x.reshape(b, seq, D_MODEL)


def kernel(x_prompt, x_sample, g_mix, w_in, w_fnet, sgu_v_gain, sgu_w, sgu_b, w_sgu,
           s5_lam_re, s5_lam_im, s5_log_dt, s5_b_re, s5_b_im, s5_c_re, s5_c_im, s5_d,
           w_glu, w_out, g_ffn, w_up, conv_w, conv_b, w_down, g_final):
    depth = w_in.shape[0]
    ws, sb = jax.vmap(_pack_sgu)(sgu_w, sgu_b)
    mend, decay, toep, rout, dtile = jax.vmap(_s5_operators)(
        s5_lam_re, s5_lam_im, s5_log_dt, s5_b_re, s5_b_im, s5_c_re, s5_c_im, s5_d)
    conv = jnp.concatenate([conv_w, conv_b[:, None], jnp.zeros((depth, 4, 2 * D_FF), F32)], axis=1)
    params = dict(
        g_mix=g_mix[:, None], w_in=w_in.astype(BF16), v_gain=sgu_v_gain[:, None],
        sgu_w=ws, sgu_b=sb, w_fnet=w_fnet.astype(BF16), w_sgu=w_sgu.astype(BF16),
        w_glu=w_glu.astype(BF16), w_out=w_out.astype(BF16),
        mend=mend, decay=decay, toep=toep, rout=rout, dtile=dtile,
        g_ffn=g_ffn[:, None], w_up=w_up.astype(BF16), conv=conv, w_down=w_down.astype(BF16),
        g_final=g_final[None])
    outs = []
    for x in (x_prompt, x_sample):
        outs.append(_trunk(x, params, _dft_consts(x.shape[1] // FFT_N2)))
    return tuple(outs)
```

```python
import functools
import math

import numpy as np
import jax
import jax.numpy as jnp
from jax import lax
from jax.experimental import pallas as pl
from jax.experimental.pallas import tpu as pltpu

F32 = jnp.float32
BF16 = jnp.bfloat16

D_MODEL = 1024
BRANCH = 512
FNET_GROUP = 128
FNET_GROUPS = BRANCH // FNET_GROUP
SGU_CHUNK = 128
SGU_HEADS = 8
SGU_HEAD_DIM = BRANCH // SGU_HEADS
S5_H = 16
S5_G = BRANCH // S5_H
S5_P = 64
S5_L = 16
S5_Q = 4
S5_GQ = S5_G // S5_Q
S5_W = S5_L * FNET_GROUP
S5_PIECES = S5_W // FNET_GROUP
S5_PAIRS = S5_GQ // 2
S5_PW = 2 * S5_L * S5_H
D_FF = 2816
FF_BLK = 256
FF_NBLK = D_FF // FF_BLK
RMS_EPS = 1e-6
FFT_N2 = 128
HALO = 16

TOKEN_TILE = 512
PROJ_TILE = 1024
FFN_TILE = 1024
VMEM_LIMIT = 60 * 1024 * 1024


def _cparams(*sem):
    return pltpu.CompilerParams(dimension_semantics=sem, vmem_limit_bytes=VMEM_LIMIT)


def _const_spec(shape):
    nd = len(shape)
    return pl.BlockSpec(shape, lambda *_: (0,) * nd)


def _layer_spec(shape, layer, index=None, resident=False):
    idx = (layer,) + ((0,) * len(shape) if index is None else tuple(index))
    kwargs = dict(pipeline_mode=pl.Buffered(1)) if resident else {}
    return pl.BlockSpec((None,) + tuple(shape), lambda *_: idx, **kwargs)


def _dot(a, b):
    return jnp.dot(a, b, preferred_element_type=F32)


def _rms(x, g):
    return x * lax.rsqrt(jnp.mean(x * x, axis=-1, keepdims=True) + RMS_EPS) * g


def _pack_complex(re, im):
    return pltpu.pack_elementwise([re, im], packed_dtype=BF16)


def _unpack_complex(w):
    re = pltpu.unpack_elementwise(w, index=0, packed_dtype=BF16, unpacked_dtype=F32)
    im = pltpu.unpack_elementwise(w, index=1, packed_dtype=BF16, unpacked_dtype=F32)
    return jnp.concatenate([re, im], axis=0).astype(BF16)


def _granule_transpose(arrs):
    lane = lax.broadcasted_iota(jnp.int32, arrs[0].shape, 1)
    a = list(arrs)
    for d in (4, 2, 1):
        upper = ((lane // S5_H) & d) != 0
        for i in range(S5_GQ):
            if i & d:
                continue
            lo, hi = a[i], a[i + d]
            a[i] = jnp.where(upper, pltpu.roll(hi, d * S5_H, 1), lo)
            a[i + d] = jnp.where(upper, hi, pltpu.roll(lo, FNET_GROUP - d * S5_H, 1))
    return a


def _proj_kernel(x_ref, g_ref, w_ref, fc_ref, vg_ref, ws_ref, sb_ref,
                 h_ref, xf_ref, sg_ref, c4_ref, c_scr, *, tm):
    h = _rms(x_ref[...], g_ref[...]).astype(BF16)
    h_ref[...] = h

    v = _dot(h, w_ref[:, 2 * BRANCH:3 * BRANCH])
    u = _dot(h, w_ref[:, BRANCH:2 * BRANCH])
    c = _dot(h, w_ref[:, 3 * BRANCH:4 * BRANCH])
    a = _dot(h, w_ref[:, 0:BRANCH]).astype(BF16)

    v = (_rms(jax.nn.gelu(v), vg_ref[...])).astype(BF16)
    u = jax.nn.gelu(u)
    lane = lax.broadcasted_iota(jnp.int32, (SGU_CHUNK, FNET_GROUP), 1)
    first_head = lane < SGU_HEAD_DIM
    zero = jnp.zeros((SGU_CHUNK, FNET_GROUP), BF16)
    for ck in range(tm // SGU_CHUNK):
        rows = slice(ck * SGU_CHUNK, (ck + 1) * SGU_CHUNK)
        for j in range(SGU_HEADS // 2):
            cols = slice(j * FNET_GROUP, (j + 1) * FNET_GROUP)
            blk = v[rows, cols]
            rhs = jnp.concatenate([jnp.where(first_head, blk, zero),
                                   jnp.where(first_head, zero, blk)], axis=0)
            mixed = _dot(ws_ref[j], rhs) + sb_ref[:, cols]
            sg_ref[rows, cols] = (u[rows, cols] * mixed).astype(BF16)

    for q in range(S5_Q):
        c_scr[q] = c[:, q * FNET_GROUP:(q + 1) * FNET_GROUP]
    for q in range(S5_Q):
        for half in range(2):
            by_time = [c_scr[q, pl.ds(S5_GQ * half + i, tm // S5_L, stride=S5_L), :]
                       for i in range(S5_GQ)]
            for gl, blk in enumerate(_granule_transpose(by_time)):
                lo = gl * 2 * FNET_GROUP + half * FNET_GROUP
                c4_ref[q, :, lo:lo + FNET_GROUP] = blk.astype(BF16)

    for g in range(FNET_GROUPS):
        lo, hi = g * FNET_GROUP, (g + 1) * FNET_GROUP
        xc = _dot(a[:, lo:hi], fc_ref[...])
        packed = _pack_complex(xc[:, :FNET_GROUP], xc[:, FNET_GROUP:])
        s1_blk = tm // FFT_N2
        slab = xf_ref.at[g, 0].reshape(FFT_N2 * s1_blk, FNET_GROUP)
        for j in range(s1_blk):
            slab[pl.ds(j, FFT_N2, stride=s1_blk), :] = packed[j * FFT_N2:(j + 1) * FFT_N2]


def _proj(x, layer, g, w, fc, vg, ws, sb, b, n1):
    t = x.shape[0]
    tm = PROJ_TILE
    s1_blk = tm // FFT_N2
    per_seq = n1 // s1_blk
    return pl.pallas_call(
        functools.partial(_proj_kernel, tm=tm),
        grid=(t // tm,),
        in_specs=[
            pl.BlockSpec((tm, D_MODEL), lambda i: (i, 0)),
            _layer_spec((1, D_MODEL), layer),
            _layer_spec((D_MODEL, 4 * BRANCH), layer, resident=True),
            _const_spec((FNET_GROUP, 2 * FNET_GROUP)),
            _layer_spec((1, BRANCH), layer),
            _layer_spec((SGU_HEADS // 2, SGU_CHUNK, 2 * SGU_CHUNK), layer),
            _layer_spec((SGU_CHUNK, BRANCH), layer),
        ],
        out_specs=[
            pl.BlockSpec((tm, D_MODEL), lambda i: (i, 0)),
            pl.BlockSpec((FNET_GROUPS, 1, FFT_N2, s1_blk, FNET_GROUP),
                         lambda i: (0, i // per_seq, 0, i % per_seq, 0)),
            pl.BlockSpec((tm, BRANCH), lambda i: (i, 0)),
            pl.BlockSpec((S5_Q, tm // S5_L, S5_W), lambda i: (0, i, 0)),
        ],
        out_shape=[
            jax.ShapeDtypeStruct((t, D_MODEL), BF16),
            jax.ShapeDtypeStruct((FNET_GROUPS, b, FFT_N2, n1, FNET_GROUP), jnp.uint32),
            jax.ShapeDtypeStruct((t, BRANCH), BF16),
            jax.ShapeDtypeStruct((S5_Q, t // S5_L, S5_W), BF16),
        ],
        scratch_shapes=[pltpu.VMEM((S5_Q, tm, FNET_GROUP), F32)],
        compiler_params=_cparams("parallel"),
        name="proj",
    )(x, g, w, fc, vg, ws, sb)


def _fft1_kernel(x_ref, f_ref, z_ref, *, n1, sb):
    for j in range(sb):
        x = jnp.concatenate([x_ref[g, 0, j] for g in range(FNET_GROUPS)], axis=1)
        z = _dot(f_ref[j], _unpack_complex(x))
        packed = _pack_complex(z[:n1], z[n1:])
        for g in range(FNET_GROUPS):
            slab = z_ref.at[g, 0].reshape(n1 * sb, FNET_GROUP)
            slab[pl.ds(j, n1, stride=sb), :] = packed[:, g * FNET_GROUP:(g + 1) * FNET_GROUP]


def _fft1(xf, f1, b, n1):
    sb = 8
    return pl.pallas_call(
        functools.partial(_fft1_kernel, n1=n1, sb=sb),
        grid=(b, FFT_N2 // sb),
        in_specs=[
            pl.BlockSpec((FNET_GROUPS, 1, sb, n1, FNET_GROUP), lambda i, j: (0, i, j, 0, 0)),
            pl.BlockSpec((sb, 2 * n1, 2 * n1), lambda i, j: (j, 0, 0)),
        ],
        out_specs=pl.BlockSpec((FNET_GROUPS, 1, n1, sb, FNET_GROUP), lambda i, j: (0, i, 0, j, 0)),
        out_shape=jax.ShapeDtypeStruct((FNET_GROUPS, b, n1, FFT_N2, FNET_GROUP), jnp.uint32),
        compiler_params=_cparams("parallel", "parallel"),
        name="fft1",
    )(xf, f1)


def _fft2_kernel(z_ref, f_ref, o_ref, *, kb):
    for k in range(kb):
        z = jnp.concatenate([z_ref[g, 0, k] for g in range(FNET_GROUPS)], axis=1)
        res = _dot(f_ref[...], _unpack_complex(z))
        for g in range(FNET_GROUPS):
            slab = o_ref.at[g, 0].reshape(FFT_N2 * kb, FNET_GROUP)
            slab[pl.ds(k, FFT_N2, stride=kb), :] = res[:, g * FNET_GROUP:(g + 1) * FNET_GROUP]


def _fft2(z, f2, b, n1):
    kb = 8
    return pl.pallas_call(
        functools.partial(_fft2_kernel, kb=kb),
        grid=(b, n1 // kb),
        in_specs=[
            pl.BlockSpec((FNET_GROUPS, 1, kb, FFT_N2, FNET_GROUP), lambda i, j: (0, i, j, 0, 0)),
            _const_spec((FFT_N2, 2 * FFT_N2)),
        ],
        out_specs=pl.BlockSpec((FNET_GROUPS, 1, FFT_N2, kb, FNET_GROUP), lambda i, j: (0, i, 0, j, 0)),
        out_shape=jax.ShapeDtypeStruct((FNET_GROUPS, b, FFT_N2, n1, FNET_GROUP), F32),
        compiler_params=_cparams("parallel", "parallel"),
        name="fft2",
    )(z, f2)


def _s5_kernel(x_ref, a_ref, m_ref, t_ref, r_ref, d_ref, y_ref, st_ref, *, nc):
    half = S5_PIECES // 2
    gw = S5_PW // 2

    for pp in range(S5_PAIRS):
        e = _dot(x_ref[0, :, pp * S5_PW:(pp + 1) * S5_PW], m_ref[pp])
        for k in range(4):
            st_ref[pl.ds(S5_PAIRS * k + pp, nc, stride=S5_PIECES), :] = (
                e[:, k * FNET_GROUP:(k + 1) * FNET_GROUP])

    arf, aif, arb, aib = a_ref[0, 0], a_ref[0, 1], a_ref[0, 2], a_ref[0, 3]

    def body(n, carry):
        hf, hb = carry
        fwd = pl.ds(pl.multiple_of(n * S5_PIECES, S5_PIECES), half)
        bwd = pl.ds(pl.multiple_of((nc - 1 - n) * S5_PIECES + half, half), half)
        ef = st_ref[fwd, :]
        eb = st_ref[bwd, :]
        st_ref[fwd, :] = hf
        st_ref[bwd, :] = hb
        hf = arf * hf + aif * pltpu.roll(hf, half // 2, 0) + ef
        hb = arb * hb + aib * pltpu.roll(hb, half // 2, 0) + eb
        return hf, hb

    zero = jnp.zeros((half, FNET_GROUP), F32)
    lax.fori_loop(0, nc, body, (zero, zero), unroll=4)

    for pp in range(S5_PAIRS):
        cols = slice(pp * S5_PW, (pp + 1) * S5_PW)
        x = x_ref[0, :, cols]
        h = jnp.concatenate([st_ref[pl.ds(S5_PAIRS * k + pp, nc, stride=S5_PIECES), :]
                             for k in range(4)], axis=1)
        y = jnp.concatenate([_dot(x[:, :gw], t_ref[2 * pp]),
                             _dot(x[:, gw:], t_ref[2 * pp + 1])], axis=1)
        y = y + _dot(h.astype(BF16), r_ref[pp]) + x.astype(F32) * d_ref[0, :, cols]
        y_ref[0, :, cols] = y.astype(BF16)


def _s5(c4, layer, decay, mend, toep, rout, dskip, b, nc):
    return pl.pallas_call(
        functools.partial(_s5_kernel, nc=nc),
        grid=(S5_Q, b),
        in_specs=[
            pl.BlockSpec((1, nc, S5_W), lambda q, i: (q, i, 0)),
            pl.BlockSpec((None, 1, 4, S5_PIECES // 2, FNET_GROUP), lambda q, i: (layer, q, 0, 0, 0)),
            pl.BlockSpec((None, S5_PAIRS, S5_PW, S5_PW), lambda q, i: (layer, q, 0, 0)),
            pl.BlockSpec((None, S5_GQ, S5_PW // 2, S5_PW // 2), lambda q, i: (layer, q, 0, 0)),
            pl.BlockSpec((None, S5_PAIRS, S5_PW, S5_PW), lambda q, i: (layer, q, 0, 0)),
            pl.BlockSpec((None, 1, 1, S5_W), lambda q, i: (layer, q, 0, 0)),
        ],
        out_specs=pl.BlockSpec((1, nc, S5_W), lambda q, i: (q, i, 0)),
        out_shape=jax.ShapeDtypeStruct((S5_Q, b * nc, S5_W), BF16),
        scratch_shapes=[pltpu.VMEM((nc * S5_PIECES, FNET_GROUP), F32)],
        compiler_params=_cparams("parallel", "parallel"),
        name="s5",
    )(c4, decay, mend, toep, rout, dskip)


def _merge_kernel(x_ref, h_ref, fo_ref, sg_ref, y4_ref, g_ref, wga_ref, wgb_ref, wgc_ref,
                  wf_ref, ws_ref, wglu_ref, wo_ref, o_ref, h2_ref, m_ref, y_scr, *, tm):
    h = h_ref[...]
    fo = jnp.concatenate([fo_ref[g] for g in range(FNET_GROUPS)], axis=1).astype(BF16)
    sg = sg_ref[...]
    for q in range(S5_Q):
        for half in range(2):
            by_group = [y4_ref[q, :, (2 * gl + half) * FNET_GROUP:
                               (2 * gl + half + 1) * FNET_GROUP].astype(F32)
                        for gl in range(S5_GQ)]
            for i, blk in enumerate(_granule_transpose(by_group)):
                y_scr[q, pl.ds(S5_GQ * half + i, tm // S5_L, stride=S5_L), :] = blk
    ys = jax.nn.gelu(jnp.concatenate([y_scr[q] for q in range(S5_Q)], axis=1)).astype(BF16)
    nb = 256
    for c0 in range(0, D_MODEL, nb):
        cs = slice(c0, c0 + nb)
        ga = jax.nn.sigmoid(_dot(h, wga_ref[:, cs]))
        gb = jax.nn.sigmoid(_dot(h, wgb_ref[:, cs]))
        gc = jax.nn.sigmoid(_dot(h, wgc_ref[:, cs]))
        ya = _dot(fo, wf_ref[:, cs])
        yb = _dot(sg, ws_ref[:, cs])
        val = _dot(ys, wglu_ref[:, cs])
        gate = _dot(ys, wglu_ref[:, D_MODEL + c0:D_MODEL + c0 + nb])
        yc = val * jax.nn.sigmoid(gate)
        m_ref[:, cs] = (ga * ya + gb * yb + gc * yc).astype(BF16)
    g = g_ref[...]
    for r0 in range(0, tm, SGU_CHUNK):
        rows = slice(r0, r0 + SGU_CHUNK)
        xo = x_ref[rows, :] + _dot(m_ref[rows, :], wo_ref[...])
        o_ref[rows, :] = xo
        h2_ref[rows, :] = _rms(xo, g).astype(BF16)


def _merge(x, h, fo, sg, y4, layer, g_ffn, w_in, wf, ws, wglu, wo):
    gate0 = 4 * BRANCH // D_MODEL
    t = x.shape[0]
    tm = TOKEN_TILE
    return pl.pallas_call(
        functools.partial(_merge_kernel, tm=tm),
        grid=(t // tm,),
        in_specs=[
            pl.BlockSpec((tm, D_MODEL), lambda i: (i, 0)),
            pl.BlockSpec((tm, D_MODEL), lambda i: (i, 0)),
            pl.BlockSpec((FNET_GROUPS, tm, FNET_GROUP), lambda i: (0, i, 0)),
            pl.BlockSpec((tm, BRANCH), lambda i: (i, 0)),
            pl.BlockSpec((S5_Q, tm // S5_L, S5_W), lambda i: (0, i, 0)),
            _layer_spec((1, D_MODEL), layer),
            _layer_spec((D_MODEL, D_MODEL), layer, index=(0, gate0)),
            _layer_spec((D_MODEL, D_MODEL), layer, index=(0, gate0 + 1)),
            _layer_spec((D_MODEL, D_MODEL), layer, index=(0, gate0 + 2)),
            _layer_spec((BRANCH, D_MODEL), layer),
            _layer_spec((BRANCH, D_MODEL), layer),
            _layer_spec((BRANCH, 2 * D_MODEL), layer),
            _layer_spec((D_MODEL, D_MODEL), layer),
        ],
        out_specs=[pl.BlockSpec((tm, D_MODEL), lambda i: (i, 0)),
                   pl.BlockSpec((tm, D_MODEL), lambda i: (i, 0))],
        out_shape=[jax.ShapeDtypeStruct((t, D_MODEL), F32),
                   jax.ShapeDtypeStruct((t, D_MODEL), BF16)],
        scratch_shapes=[pltpu.VMEM((tm, D_MODEL), BF16),
                        pltpu.VMEM((S5_Q, tm, FNET_GROUP), F32)],
        compiler_params=_cparams("parallel"),
        name="merge",
    )(x, h, fo, sg, y4, g_ffn, w_in, w_in, w_in, wf, ws, wglu, wo)


def _ffn_kernel(hp_ref, hm_ref, hn_ref, x_ref, wup_ref, cw_ref, wdn_ref, gf_ref,
                o_ref, h_ref, up_ref, act_ref, *, tm, seq, final):
    pos = lax.rem(pl.program_id(0) * tm, seq)
    zero = jnp.zeros((HALO, D_MODEL), BF16)
    h_ref[0:HALO] = jnp.where(pos != 0, hp_ref[...], zero)
    h_ref[HALO:HALO + tm] = hm_ref[...]
    h_ref[HALO + tm:] = jnp.where(pos + tm != seq, hn_ref[...], zero)
    hext = h_ref[...]
    rows = tm + 2 * HALO
    for j in range(FF_NBLK):
        gate_cols = slice(j * FF_BLK, (j + 1) * FF_BLK)
        val_cols = slice(D_FF + j * FF_BLK, D_FF + (j + 1) * FF_BLK)
        up = up_ref.at[j % 2]
        up[:, :FF_BLK] = _dot(hext, wup_ref[:, gate_cols])
        up[:, FF_BLK:] = _dot(hext, wup_ref[:, val_cols])
        cw = jnp.concatenate([cw_ref[:, gate_cols], cw_ref[:, val_cols]], axis=1)
        u = up[...]
        conv = (pltpu.roll(u, 1, 0) * cw[0:1] + u * cw[1:2]
                + pltpu.roll(u, rows - 1, 0) * cw[2:3] + cw[3:4])[HALO:HALO + tm]
        gate = conv[:, :FF_BLK]
        half = 0.5 * gate
        act = (half + half * jnp.tanh(half)) * conv[:, FF_BLK:]
        act_ref[:, j * FF_BLK:(j + 1) * FF_BLK] = act.astype(BF16)
    y = x_ref[...] + _dot(act_ref[...], wdn_ref[...])
    if final:
        y = _rms(y, gf_ref[...])
    o_ref[...] = y


def _ffn(x, h, layer, wup, cw, wdn, gf, seq, final):
    t = x.shape[0]
    tm = FFN_TILE
    per = tm // HALO
    last = t // HALO - 1
    rows = tm + 2 * HALO
    return pl.pallas_call(
        functools.partial(_ffn_kernel, tm=tm, seq=seq, final=final),
        grid=(t // tm,),
        in_specs=[
            pl.BlockSpec((HALO, D_MODEL), lambda i: (jnp.maximum(i * per - 1, 0), 0)),
            pl.BlockSpec((tm, D_MODEL), lambda i: (i, 0)),
            pl.BlockSpec((HALO, D_MODEL), lambda i: (jnp.minimum((i + 1) * per, last), 0)),
            pl.BlockSpec((tm, D_MODEL), lambda i: (i, 0)),
            _layer_spec((D_MODEL, 2 * D_FF), layer, resident=True),
            _layer_spec((8, 2 * D_FF), layer),
            _layer_spec((D_FF, D_MODEL), layer, resident=True),
            _const_spec((1, D_MODEL)),
        ],
        out_specs=pl.BlockSpec((tm, D_MODEL), lambda i: (i, 0)),
        out_shape=jax.ShapeDtypeStruct((t, D_MODEL), F32),
        scratch_shapes=[pltpu.VMEM((rows, D_MODEL), BF16),
                        pltpu.VMEM((2, rows, 2 * FF_BLK), F32),
                        pltpu.VMEM((tm, D_FF), BF16)],
        compiler_params=_cparams("parallel"),
        name="ffn_final" if final else "ffn",
    )(h, h, h, x, wup, cw, wdn, gf)


def _dft_consts(n1):
    n2 = FFT_N2
    s = n1 * n2
    k = np.arange(FNET_GROUP)
    ang = 2 * np.pi * ((k[:, None] * k[None, :]) % FNET_GROUP) / FNET_GROUP
    fc = np.concatenate([np.cos(ang), -np.sin(ang)], axis=1) / math.sqrt(FNET_GROUP)
    k1 = np.arange(n1)[None, :, None]
    s1 = np.arange(n1)[None, None, :]
    s2 = np.arange(n2)[:, None, None]
    ang = 2 * np.pi * ((k1 * (n2 * s1 + s2)) % s) / s
    c, sn = np.cos(ang), np.sin(ang)
    f1 = np.concatenate([np.concatenate([c, sn], axis=2),
                         np.concatenate([-sn, c], axis=2)], axis=1) / math.sqrt(n1)
    k2 = np.arange(n2)
    ang = 2 * np.pi * ((k2[:, None] * k2[None, :]) % n2) / n2
    f2 = np.concatenate([np.cos(ang), np.sin(ang)], axis=1) / math.sqrt(n2)
    return (jnp.asarray(fc, BF16), jnp.asarray(f1, BF16), jnp.asarray(f2, BF16))


def _cmul(ar, ai, br, bi):
    return ar * br - ai * bi, ar * bi + ai * br


def _s5_operators(lam_re, lam_im, log_dt, b_re, b_im, c_re, c_im, d_skip):
    L, G, P, H, Q, GQ = S5_L, S5_G, S5_P, S5_H, S5_Q, S5_GQ
    dt = jnp.exp(log_dt)[..., None]
    lr, li = lam_re * dt, lam_im * dt

    def lb_pow(n):
        n = jnp.asarray(n, F32)[:, None, None, None]
        mag = jnp.exp(n * lr)
        return mag * jnp.cos(n * li), mag * jnp.sin(n * li)

    lbr, lbi = lb_pow([1.0])
    den = lam_re * lam_re + lam_im * lam_im
    bsr, bsi = _cmul(lbr[0] - 1.0, lbi[0], lam_re / den, -lam_im / den)
    cbr, cbi = _cmul(c_re, c_im, bsr[:, :, None, :], bsi[:, :, None, :])

    pw_r, pw_i = lb_pow(np.arange(L + 1))

    def lag_kernel(d):
        wr, wi = _cmul(cbr[d][None], cbi[d][None],
                       pw_r[:L, d][:, :, None, :], pw_i[:L, d][:, :, None, :])
        return (jnp.einsum('sghp,gpk->sghk', wr, b_re)
                - jnp.einsum('sghp,gpk->sghk', wi, b_im))

    kf, kb = lag_kernel(0), lag_kernel(1)
    klag = jnp.concatenate([kb[1:][::-1], (kf[0] + kb[0])[None], kf[1:]], axis=0)
    idx = np.arange(L)[None, :] - np.arange(L)[:, None] + (L - 1)
    kss = klag[idx].astype(BF16)
    gw = L * H
    toep = jnp.transpose(kss, (2, 0, 4, 1, 3)).reshape(G, gw, gw)
    pair_eye = jnp.eye(2, dtype=BF16)

    def end_state(d, exps):
        er, ei = pw_r[exps, d], pw_i[exps, d]
        return _cmul(er[..., None], ei[..., None], b_re[None], b_im[None])

    fr, fi = end_state(0, np.arange(L)[::-1])
    br_, bi_ = end_state(1, np.arange(L))
    parts = jnp.stack([fr, fi, br_, bi_], axis=0).astype(BF16)
    parts = jnp.transpose(parts, (2, 1, 4, 0, 3)).reshape(G // 2, 2, gw, 4, P)
    mend = jnp.einsum('ngrcp,gk->ngrckp', parts, pair_eye).reshape(G // 2, 2 * gw, 4 * 2 * P)

    def read_out(d, exps):
        wr, wi = _cmul(cbr[d][None], cbi[d][None],
                       pw_r[exps, d][:, :, None, :], pw_i[exps, d][:, :, None, :])
        return wr, -wi

    rfr, rfi = read_out(0, np.arange(1, L + 1))
    rbr, rbi = read_out(1, np.arange(L, 0, -1))
    parts = jnp.stack([rfr, rfi, rbr, rbi], axis=0).astype(BF16)
    parts = jnp.transpose(parts, (2, 0, 4, 1, 3)).reshape(G // 2, 2, 4, P, gw)
    rout = jnp.einsum('ngcpr,gk->ncgpkr', parts, pair_eye).reshape(G // 2, 4 * 2 * P, 2 * gw)

    ar = pw_r[L].reshape(2, Q, GQ * P // FNET_GROUP, FNET_GROUP)
    ai = pw_i[L].reshape(2, Q, GQ * P // FNET_GROUP, FNET_GROUP)
    a_re = jnp.concatenate([ar, ar], axis=2)
    a_im = jnp.concatenate([-ai, ai], axis=2)
    decay = jnp.stack([a_re[0], a_im[0], a_re[1], a_im[1]], axis=1)

    dtile = jnp.broadcast_to(d_skip[:, None, :], (G, L, H)).reshape(Q, 1, S5_W)
    return mend.astype(BF16), decay, toep.astype(BF16), rout.astype(BF16), dtile


def _pack_sgu(sgu_w, sgu_b):
    ws = jnp.concatenate([sgu_w[0::2], sgu_w[1::2]], axis=2).astype(BF16)
    sb = jnp.repeat(sgu_b.T, SGU_HEAD_DIM, axis=1)
    return ws, sb


def _trunk(x, p, consts):
    b, seq, _ = x.shape
    n1 = seq // FFT_N2
    nc = seq // S5_L
    t = b * seq
    fc, f1, f2 = consts
    depth = p["w_in"].shape[0]
    x = x.reshape(t, D_MODEL)
    for li in range(depth):
        h, xf, sg, c4 = _proj(x, li, p["g_mix"], p["w_in"], fc, p["v_gain"], p["sgu_w"],
                              p["sgu_b"], b, n1)
        fo = _fft2(_fft1(xf, f1, b, n1), f2, b, n1).reshape(FNET_GROUPS, t, FNET_GROUP)
        y4 = _s5(c4, li, p["decay"], p["mend"], p["toep"], p["rout"], p["dtile"], b, nc)
        x, h2 = _merge(x, h, fo, sg, y4, li, p["g_ffn"], p["w_in"], p["w_fnet"], p["w_sgu"],
                       p["w_glu"], p["w_out"])
        x = _ffn(x, h2, li, p["w_up"], p["conv"], p["w_down"], p["g_final"], seq,
                 final=(li == depth - 1))
    return x.reshape(b, seq, D_MODEL)


def kernel(x_prompt, x_sample, g_mix, w_in, w_fnet, sgu_v_gain, sgu_w, sgu_b, w_sgu,
           s5_lam_re, s5_lam_im, s5_log_dt, s5_b_re, s5_b_im, s5_c_re, s5_c_im, s5_d,
           w_glu, w_out, g_ffn, w_up, conv_w, conv_b, w_down, g_final):
    depth = w_in.shape[0]
    ws, sb = jax.vmap(_pack_sgu)(sgu_w, sgu_b)
    mend, decay, toep, rout, dtile = jax.vmap(_s5_operators)(
        s5_lam_re, s5_lam_im, s5_log_dt, s5_b_re, s5_b_im, s5_c_re, s5_c_im, s5_d)
    conv = jnp.concatenate([conv_w, conv_b[:, None], jnp.zeros((depth, 4, 2 * D_FF), F32)], axis=1)
    params = dict(
        g_mix=g_mix[:, None], w_in=w_in.astype(BF16), v_gain=sgu_v_gain[:, None],
        sgu_w=ws, sgu_b=sb, w_fnet=w_fnet.astype(BF16), w_sgu=w_sgu.astype(BF16),
        w_glu=w_glu.astype(BF16), w_out=w_out.astype(BF16),
        mend=mend, decay=decay, toep=toep, rout=rout, dtile=dtile,
        g_ffn=g_ffn[:, None], w_up=w_up.astype(BF16), conv=conv, w_down=w_down.astype(BF16),
        g_final=g_final[None])
    outs = []
    for x in (x_prompt, x_sample):
        outs.append(_trunk(x, params, _dft_consts(x.shape[1] // FFT_N2)))
    return tuple(outs)
```
